```python
import math
import jax
import jax.numpy as jnp
from jax import lax
import numpy as np

D_MODEL = 1024
BATCH = 1
SEQ = 16384
DEPTH = 4
DEC_BATCH = 8
DEC_SEQ = 64
PAST_LEN = 2048

CHUNK = 64
D_PLE = 256
ML_HEADS = 4
ML_DQK = 64
ML_DV = 128
ML_W = ML_HEADS * ML_DV
RW_HEADS = 8
RW_DH = 64
RW_W = RW_HEADS * RW_DH
RW_W_RANK = 64
RW_A_RANK = 64
RW_G_RANK = 128
GD_HEADS = 4
GD_DK = 128
GD_DV = 128
GD_W = GD_HEADS * GD_DV
GD_CONV = 4
N_GROUPS = 4
EXP_PER_GROUP = 4
N_EXPERTS = N_GROUPS * EXP_PER_GROUP
D_EXPERT = 256
TOP_K = 2
ML_COLS = 2 * ML_HEADS * ML_DQK + 2 * ML_W + 2 * ML_HEADS
RW_COLS = 3 * RW_W + RW_W_RANK + RW_A_RANK + RW_G_RANK
GD_CONV_CH = 2 * GD_HEADS * GD_DK + GD_W
GD_COLS = GD_CONV_CH + GD_W + 2 * GD_HEADS
GATE_COLS = 3 * D_MODEL
IN_COLS = ML_COLS + RW_COLS + GD_COLS + GATE_COLS
DEEPNORM_ALPHA = (2 * DEPTH) ** 0.25
DEEPNORM_BETA = (8 * DEPTH) ** -0.25
LN_EPS = 1e-5
GN_EPS = 64e-5
NEG_LARGE = -1e30

kernel_name = "hybrid_mlstm_rwkv7_gdn_hmoe_stream_step"


def _split(x, sizes):
    offs = np.cumsum((0,) + tuple(sizes))
    return [x[..., int(offs[i]):int(offs[i + 1])] for i in range(len(sizes))]


def _pad_time(x, t_pad, value=0.0):
    pad = t_pad - x.shape[1]
    if pad == 0:
        return x
    widths = [(0, 0)] * x.ndim
    widths[1] = (0, pad)
    return jnp.pad(x, widths, constant_values=value)


def _to_chunks(x):
    b, tp, h = x.shape[:3]
    x = x.reshape((b, tp // CHUNK, CHUNK, h) + x.shape[3:])
    return jnp.moveaxis(x, (1, 2), (0, 3))


def _from_chunks(y):
    y = jnp.moveaxis(y, (0, 3), (1, 2))
    return y.reshape((y.shape[0], y.shape[1] * y.shape[2]) + y.shape[3:])


def layer_norm(x, g, b):
    xf = x.astype(jnp.float32)
    mu = jnp.mean(xf, axis=-1, keepdims=True)
    var = jnp.mean(jnp.square(xf - mu), axis=-1, keepdims=True)
    return ((xf - mu) * lax.rsqrt(var + LN_EPS) * g + b).astype(x.dtype)


def _rms_heads(x, gain, eps=1e-6):
    xf = x.astype(jnp.float32)
    return xf * lax.rsqrt(jnp.mean(xf * xf, axis=-1, keepdims=True) + eps) * gain


def _l2norm(x, eps=1e-6):
    return x * lax.rsqrt(jnp.sum(x * x, axis=-1, keepdims=True) + eps)


def _mlstm_chunk(carry, inp):
    c, n, m = carry
    q, k, v, log_i, log_f = inp
    causal = jnp.tril(jnp.ones((CHUNK, CHUNK), dtype=bool))
    b = jnp.cumsum(log_f, axis=-1)
    d_intra = jnp.where(causal, b[..., :, None] - b[..., None, :] + log_i[..., None, :], -jnp.inf)
    d_inter = b + m[..., None]
    m_t = jnp.maximum(d_inter, jnp.max(d_intra, axis=-1))
    s = jnp.einsum('bhtd,bhsd->bhts', q, k) * jnp.exp(d_intra - m_t[..., None])
    w_inter = jnp.exp(d_inter - m_t)
    num = w_inter[..., None] * jnp.einsum('bhtd,bhde->bhte', q, c) + jnp.einsum('bhts,bhse->bhte', s, v)
    den = w_inter * jnp.einsum('bhtd,bhd->bht', q, n) + jnp.sum(s, axis=-1)
    h = num / jnp.maximum(jnp.abs(den), jnp.exp(-m_t))[..., None]
    b_last = b[..., -1]
    d_last = b_last[..., None] - b + log_i
    m_new = jnp.maximum(b_last + m, jnp.max(d_last, axis=-1))
    w_last = jnp.exp(d_last - m_new[..., None])
    decay = jnp.exp(b_last + m - m_new)
    c_new = decay[..., None, None] * c + jnp.einsum('bhs,bhsd,bhse->bhde', w_last, k, v)
    n_new = decay[..., None] * n + jnp.einsum('bhs,bhsd->bhd', w_last, k)
    return (c_new, n_new, m_new), h


def mlstm_mixer(cols, c0, n0, m0, i_bias, f_bias, norm_g):
    bsz, t, _ = cols.shape
    cols = cols.astype(jnp.float32)
    q, k, v, o, ig, fg = _split(cols, (ML_HEADS * ML_DQK, ML_HEADS * ML_DQK, ML_W, ML_W, ML_HEADS, ML_HEADS))
    t_pad = -(-t // CHUNK) * CHUNK
    q = _pad_time(q.reshape(bsz, t, ML_HEADS, ML_DQK) * ML_DQK ** -0.5, t_pad)
    k = _pad_time(k.reshape(bsz, t, ML_HEADS, ML_DQK), t_pad)
    v = _pad_time(v.reshape(bsz, t, ML_HEADS, ML_DV), t_pad)
    log_i = _pad_time(ig + i_bias, t_pad, NEG_LARGE)
    log_f = _pad_time(jax.nn.log_sigmoid(fg + f_bias), t_pad)
    xs = (_to_chunks(q), _to_chunks(k), _to_chunks(v), _to_chunks(log_i), _to_chunks(log_f))
    init = (c0.astype(jnp.float32), n0.astype(jnp.float32), m0.astype(jnp.float32))
    (c1, n1, m1), h = lax.scan(_mlstm_chunk, init, xs)
    h = _from_chunks(h)[:, :t]
    h = _rms_heads(h, norm_g.reshape(ML_HEADS, ML_DV)) * jax.nn.sigmoid(o).reshape(bsz, t, ML_HEADS, ML_DV)
    return h.reshape(bsz, t, ML_W), (c1, n1, m1)


def _rwkv7_step(s, inp):
    r, w, k, v, a, b = inp
    sa = jnp.einsum('bhvk,bhk->bhv', s, a)
    s = s * w[:, :, None, :] + sa[..., None] * b[:, :, None, :] + v[..., None] * k[:, :, None, :]
    y = jnp.einsum('bhvk,bhk->bhv', s, r)
    return s, y


def rwkv7_mixer(cols, shift0, s0, mu, w0, w2, a0, a2, g2, k_k, k_a, r_k, gn_g, gn_b):
    bsz, t, _ = cols.shape
    cols = cols.astype(jnp.float32)
    prev = jnp.concatenate([shift0.astype(jnp.float32), cols[:, :-1]], axis=1)
    mixed = cols + (prev - cols) * mu
    r, k, v, w_lo, a_lo, g_lo = _split(mixed, (RW_W, RW_W, RW_W, RW_W_RANK, RW_A_RANK, RW_G_RANK))
    w_log = -jax.nn.softplus(-(w0 + jnp.tanh(w_lo) @ w2)) - 0.5
    decay = jnp.exp(-jnp.exp(w_log))
    a = jax.nn.sigmoid(a0 + a_lo @ a2)
    g = jax.nn.sigmoid(g_lo) @ g2
    shp = (bsz, t, RW_HEADS, RW_DH)
    r, k, v, decay, a = r.reshape(shp), k.reshape(shp), v.reshape(shp), decay.reshape(shp), a.reshape(shp)
    kk = _l2norm(k * k_k.reshape(RW_HEADS, RW_DH))
    k = k * (1.0 + (a - 1.0) * k_a.reshape(RW_HEADS, RW_DH))
    xs = tuple(jnp.moveaxis(z, 1, 0) for z in (r, decay, k, v, -kk, kk * a))
    s1, y = lax.scan(_rwkv7_step, s0.astype(jnp.float32), xs)
    y = jnp.moveaxis(y, 0, 1)
    mu_y = jnp.mean(y, axis=-1, keepdims=True)
    var_y = jnp.mean(jnp.square(y - mu_y), axis=-1, keepdims=True)
    y = (y - mu_y) * lax.rsqrt(var_y + GN_EPS) * gn_g.reshape(RW_HEADS, RW_DH) + gn_b.reshape(RW_HEADS, RW_DH)
    y = y + jnp.sum(r * k * r_k, axis=-1, keepdims=True) * v
    y = y.reshape(bsz, t, RW_W) * g
    return y, (s1, cols[:, t - 1:])


def _gdn_chunk(s, inp):
    q, k, v, beta, log_a = inp
    incl = jnp.tril(jnp.ones((CHUNK, CHUNK), dtype=bool))
    strict = jnp.tril(jnp.ones((CHUNK, CHUNK), dtype=bool), -1)
    gc = jnp.cumsum(log_a, axis=-1)
    dec = jnp.exp(jnp.where(incl, gc[..., :, None] - gc[..., None, :], -jnp.inf))
    a_mat = jnp.where(strict, beta[..., None] * jnp.einsum('bhtd,bhsd->bhts', k, k) * dec, 0.0)
    rhs = jnp.concatenate([v * beta[..., None], k * (beta * jnp.exp(gc))[..., None]], axis=-1)
    sol = lax.linalg.triangular_solve(jnp.eye(CHUNK, dtype=a_mat.dtype) + a_mat, rhs, left_side=True, lower=True)
    u = sol[..., :GD_DV] - jnp.einsum('bhtd,bhde->bhte', sol[..., GD_DV:], s)
    qk = jnp.einsum('bhtd,bhsd->bhts', q, k) * dec
    o = jnp.einsum('bhtd,bhde->bhte', q * jnp.exp(gc)[..., None], s) + jnp.einsum('bhts,bhse->bhte', qk, u)
    g_last = gc[..., -1]
    s_new = jnp.exp(g_last)[..., None, None] * s + jnp.einsum('bhsd,bhse->bhde', k * jnp.exp(g_last[..., None] - gc)[..., None], u)
    return s_new, o


def gdn_mixer(cols, conv0, s0, conv_w, a_log, dt_bias, norm_g):
    bsz, t, _ = cols.shape
    cols = cols.astype(jnp.float32)
    qkv, z, beta_pre, a_pre = _split(cols, (GD_CONV_CH, GD_W, GD_HEADS, GD_HEADS))
    xp = jnp.concatenate([conv0.astype(jnp.float32), qkv], axis=1)
    conv = lax.conv_general_dilated(xp, conv_w.astype(jnp.float32)[:, None, :], window_strides=(1,),
                                    padding='VALID', dimension_numbers=('NWC', 'WIO', 'NWC'),
                                    feature_group_count=GD_CONV_CH)
    conv = jax.nn.silu(conv)
    new_conv = xp[:, xp.shape[1] - (GD_CONV - 1):]
    q, k, v = _split(conv, (GD_HEADS * GD_DK, GD_HEADS * GD_DK, GD_W))
    t_pad = -(-t // CHUNK) * CHUNK
    q = _pad_time(_l2norm(q.reshape(bsz, t, GD_HEADS, GD_DK)) * GD_DK ** -0.5, t_pad)
    k = _pad_time(_l2norm(k.reshape(bsz, t, GD_HEADS, GD_DK)), t_pad)
    v = _pad_time(v.reshape(bsz, t, GD_HEADS, GD_DV), t_pad)
    beta = _pad_time(jax.nn.sigmoid(beta_pre), t_pad)
    log_a = _pad_time(-jnp.exp(a_log) * jax.nn.softplus(a_pre + dt_bias), t_pad)
    xs = (_to_chunks(q), _to_chunks(k), _to_chunks(v), _to_chunks(beta), _to_chunks(log_a))
    s1, o = lax.scan(_gdn_chunk, s0.astype(jnp.float32), xs)
    o = _from_chunks(o)[:, :t]
    o = _rms_heads(o, norm_g) * jax.nn.silu(z).reshape(bsz, t, GD_HEADS, GD_DV)
    return o.reshape(bsz, t, GD_W), (s1, new_conv)


def hier_moe(x, w_rg, b_rg, w_re, b_re, e_gate, e_up, e_down):
    bsz, t, _ = x.shape
    p_group = jax.nn.softmax((x @ w_rg + b_rg).astype(jnp.float32), axis=-1)
    g_sel = jnp.argmax(p_group, axis=-1)
    p_sel = jnp.max(p_group, axis=-1, keepdims=True)
    e_logits = (x @ w_re + b_re).astype(jnp.float32).reshape(bsz, t, N_GROUPS, EXP_PER_GROUP)
    e_in = jnp.einsum('btge,btg->bte', e_logits, jax.nn.one_hot(g_sel, N_GROUPS, dtype=jnp.float32))
    top_p, top_i = lax.top_k(jax.nn.softmax(e_in, axis=-1), TOP_K)
    top_p = top_p / jnp.sum(top_p, axis=-1, keepdims=True) * p_sel
    expert_idx = g_sel[..., None] * EXP_PER_GROUP + top_i
    gate = jnp.sum(jax.nn.one_hot(expert_idx, N_EXPERTS, dtype=jnp.float32) * top_p[..., None], axis=-2)
    h = jax.nn.silu(jnp.einsum('btd,edf->btef', x, e_gate)) * jnp.einsum('btd,edf->btef', x, e_up)
    h = h * gate[..., None].astype(h.dtype)
    return jnp.einsum('btef,efd->btd', h, e_down)


def trunk_layer(x, pe, state, lw):
    ml_c, ml_n, ml_m, rw_s, rw_shift, gd_s, gd_conv = state
    bsz, t, _ = x.shape
    cols = x @ lw['w_in']
    c_ml, c_rw, c_gd, c_gate = _split(cols, (ML_COLS, RW_COLS, GD_COLS, GATE_COLS))
    h_ml, (ml_c, ml_n, ml_m) = mlstm_mixer(c_ml, ml_c, ml_n, ml_m, lw['ml_i_bias'], lw['ml_f_bias'], lw['ml_norm'])
    h_rw, (rw_s, rw_shift) = rwkv7_mixer(c_rw, rw_shift, rw_s, lw['rw_mu'], lw['rw_w0'], lw['rw_w2'], lw['rw_a0'],
                                         lw['rw_a2'], lw['rw_g2'], lw['rw_k_k'], lw['rw_k_a'], lw['rw_r_k'],
                                         lw['rw_gn_g'], lw['rw_gn_b'])
    h_gd, (gd_s, gd_conv) = gdn_mixer(c_gd, gd_conv, gd_s, lw['gd_conv_w'], lw['gd_a_log'], lw['gd_dt_bias'], lw['gd_norm'])
    gates = jax.nn.sigmoid(c_gate.reshape(bsz, t, 3, D_MODEL) + lw['b_gates'])
    merged = (gates[:, :, 0] * (h_ml.astype(x.dtype) @ lw['w_br_ml'])
              + gates[:, :, 1] * (h_rw.astype(x.dtype) @ lw['w_br_rw'])
              + gates[:, :, 2] * (h_gd.astype(x.dtype) @ lw['w_br_gd']))
    x = layer_norm(DEEPNORM_ALPHA * x + merged @ lw['w_out'], lw['ln1_g'], lw['ln1_b'])
    ffn = hier_moe(x, lw['w_rg'], lw['b_rg'], lw['w_re'], lw['b_re'], lw['e_gate'], lw['e_up'], lw['e_down'])
    ple = jax.nn.sigmoid(x @ lw['ple_gate_w'] + lw['ple_gate_b']) * (pe @ lw['ple_w'])
    x = layer_norm(DEEPNORM_ALPHA * x + ffn + ple, lw['ln2_g'], lw['ln2_b'])
    return x, (ml_c, ml_n, ml_m, rw_s, rw_shift, gd_s, gd_conv)


def setup_inputs(seed: int = 0) -> dict:
    key = jax.random.key(seed)
    ks = iter(jax.random.split(key, 64))
    f32 = jnp.float32
    L, D = DEPTH, D_MODEL

    def nrm(shape, scale):
        return jax.random.normal(next(ks), shape, f32) * scale

    def unif(shape, lo, hi):
        return jax.random.uniform(next(ks), shape, f32, lo, hi)

    dt = jnp.exp(unif((L, GD_HEADS), math.log(1e-3), math.log(1e-1)))
    return {
        'x_prompt': nrm((BATCH, SEQ, D), 1.0),
        'x_sample': nrm((DEC_BATCH, DEC_SEQ, D), 1.0),
        'p_prompt': nrm((L, BATCH, SEQ, D_PLE), 1.0),
        'p_sample': nrm((L, DEC_BATCH, DEC_SEQ, D_PLE), 1.0),
        'state_mlstm_c': nrm((L, DEC_BATCH, ML_HEADS, ML_DQK, ML_DV), 0.1),
        'state_mlstm_n': nrm((L, DEC_BATCH, ML_HEADS, ML_DQK), 0.1),
        'state_mlstm_m': nrm((L, DEC_BATCH, ML_HEADS), 0.5),
        'state_rwkv_s': nrm((L, DEC_BATCH, RW_HEADS, RW_DH, RW_DH), 0.1),
        'state_rwkv_shift': nrm((L, DEC_BATCH, 1, RW_COLS), 1.0),
        'state_gdn_s': nrm((L, DEC_BATCH, GD_HEADS, GD_DK, GD_DV), 0.1),
        'state_gdn_conv': nrm((L, DEC_BATCH, GD_CONV - 1, GD_CONV_CH), 1.0),
        'w_in': nrm((L, D, IN_COLS), D ** -0.5),
        'ml_i_bias': nrm((L, ML_HEADS), 0.1),
        'ml_f_bias': 3.0 + nrm((L, ML_HEADS), 0.5),
        'ml_norm': 1.0 + nrm((L, ML_W), 0.02),
        'rw_mu': unif((L, RW_COLS), 0.0, 1.0),
        'rw_w0': unif((L, RW_W), -6.5, -1.5),
        'rw_w2': nrm((L, RW_W_RANK, RW_W), 0.1),
        'rw_a0': nrm((L, RW_W), 0.1),
        'rw_a2': nrm((L, RW_A_RANK, RW_W), RW_A_RANK ** -0.5),
        'rw_g2': nrm((L, RW_G_RANK, RW_W), RW_G_RANK ** -0.5),
        'rw_k_k': 0.85 + nrm((L, RW_W), 0.02),
        'rw_k_a': 1.0 + nrm((L, RW_W), 0.02),
        'rw_r_k': nrm((L, RW_HEADS, RW_DH), 0.1),
        'rw_gn_g': 1.0 + nrm((L, RW_W), 0.02),
        'rw_gn_b': nrm((L, RW_W), 0.02),
        'gd_conv_w': nrm((L, GD_CONV, GD_CONV_CH), GD_CONV ** -0.5),
        'gd_a_log': jnp.log(unif((L, GD_HEADS), 1.0, 16.0)),
        'gd_dt_bias': dt + jnp.log(-jnp.expm1(-dt)),
        'gd_norm': 1.0 + nrm((L, GD_DV), 0.02),
        'w_br_ml': nrm((L, ML_W, D), ML_W ** -0.5),
        'w_br_rw': nrm((L, RW_W, D), RW_W ** -0.5),
        'w_br_gd': nrm((L, GD_W, D), GD_W ** -0.5),
        'b_gates': nrm((L, 3, D), 0.1),
        'w_out': nrm((L, D, D), D ** -0.5 * DEEPNORM_BETA),
        'ln1_g': 1.0 + nrm((L, D), 0.02),
        'ln1_b': nrm((L, D), 0.02),
        'w_rg': nrm((L, D, N_GROUPS), D ** -0.5),
        'b_rg': nrm((L, N_GROUPS), 0.01),
        'w_re': nrm((L, D, N_EXPERTS), D ** -0.5),
        'b_re': nrm((L, N_EXPERTS), 0.01),
        'e_gate': nrm((L, N_EXPERTS, D, D_EXPERT), D ** -0.5),
        'e_up': nrm((L, N_EXPERTS, D, D_EXPERT), D ** -0.5),
        'e_down': nrm((L, N_EXPERTS, D_EXPERT, D), D_EXPERT ** -0.5 * DEEPNORM_BETA),
        'ple_w': nrm((L, D_PLE, D), D_PLE ** -0.5 * DEEPNORM_BETA),
        'ple_gate_w': nrm((L, D, D), D ** -0.5),
        'ple_gate_b': nrm((L, D), 0.1),
        'ln2_g': 1.0 + nrm((L, D), 0.02),
        'ln2_b': nrm((L, D), 0.02),
    }


def reference(x_prompt, x_sample, p_prompt, p_sample,
              state_mlstm_c, state_mlstm_n, state_mlstm_m, state_rwkv_s, state_rwkv_shift, state_gdn_s, state_gdn_conv,
              w_in, ml_i_bias, ml_f_bias, ml_norm,
              rw_mu, rw_w0, rw_w2, rw_a0, rw_a2, rw_g2, rw_k_k, rw_k_a, rw_r_k, rw_gn_g, rw_gn_b,
              gd_conv_w, gd_a_log, gd_dt_bias, gd_norm,
              w_br_ml, w_br_rw, w_br_gd, b_gates, w_out, ln1_g, ln1_b,
              w_rg, b_rg, w_re, b_re, e_gate, e_up, e_down,
              ple_w, ple_gate_w, ple_gate_b, ln2_g, ln2_b):
    f32 = jnp.float32
    bp = x_prompt.shape[0]
    zero_state = (jnp.zeros((bp, ML_HEADS, ML_DQK, ML_DV), f32), jnp.zeros((bp, ML_HEADS, ML_DQK), f32),
                  jnp.zeros((bp, ML_HEADS), f32), jnp.zeros((bp, RW_HEADS, RW_DH, RW_DH), f32),
                  jnp.zeros((bp, 1, RW_COLS), f32), jnp.zeros((bp, GD_HEADS, GD_DK, GD_DV), f32),
                  jnp.zeros((bp, GD_CONV - 1, GD_CONV_CH), f32))
    x_p, x_s = x_prompt, x_sample
    new_p = [[] for _ in range(7)]
    new_s = [[] for _ in range(7)]
    for l in range(DEPTH):
        lw = {'w_in': w_in[l], 'ml_i_bias': ml_i_bias[l], 'ml_f_bias': ml_f_bias[l], 'ml_norm': ml_norm[l],
              'rw_mu': rw_mu[l], 'rw_w0': rw_w0[l], 'rw_w2': rw_w2[l], 'rw_a0': rw_a0[l], 'rw_a2': rw_a2[l],
              'rw_g2': rw_g2[l], 'rw_k_k': rw_k_k[l], 'rw_k_a': rw_k_a[l], 'rw_r_k': rw_r_k[l],
              'rw_gn_g': rw_gn_g[l], 'rw_gn_b': rw_gn_b[l], 'gd_conv_w': gd_conv_w[l], 'gd_a_log': gd_a_log[l],
              'gd_dt_bias': gd_dt_bias[l], 'gd_norm': gd_norm[l], 'w_br_ml': w_br_ml[l], 'w_br_rw': w_br_rw[l],
              'w_br_gd': w_br_gd[l], 'b_gates': b_gates[l], 'w_out': w_out[l], 'ln1_g': ln1_g[l], 'ln1_b': ln1_b[l],
              'w_rg': w_rg[l], 'b_rg': b_rg[l], 'w_re': w_re[l], 'b_re': b_re[l], 'e_gate': e_gate[l],
              'e_up': e_up[l], 'e_down': e_down[l], 'ple_w': ple_w[l], 'ple_gate_w': ple_gate_w[l],
              'ple_gate_b': ple_gate_b[l], 'ln2_g': ln2_g[l], 'ln2_b': ln2_b[l]}
        x_p, st_p = trunk_layer(x_p, p_prompt[l], zero_state, lw)
        st_in = (state_mlstm_c[l], state_mlstm_n[l], state_mlstm_m[l], state_rwkv_s[l], state_rwkv_shift[l],
                 state_gdn_s[l], state_gdn_conv[l])
        x_s, st_s = trunk_layer(x_s, p_sample[l], st_in, lw)
        for j in range(7):
            new_p[j].append(st_p[j])
            new_s[j].append(st_s[j])
    mlc_p, mln_p, mlm_p, rws_p, rwsh_p, gds_p, gdc_p = [jnp.stack(z) for z in new_p]
    mlc_s, mln_s, mlm_s, rws_s, rwsh_s, gds_s, gdc_s = [jnp.stack(z) for z in new_s]
    return (x_p, x_s, mlc_p, mln_p, mlm_p, rws_p, rwsh_p, gds_p, gdc_p,
            mlc_s, mln_s, mlm_s, rws_s, rwsh_s, gds_s, gdc_s)
```

```python
import functools

import jax
import jax.numpy as jnp
from jax import lax
from jax.experimental import pallas as pl
from jax.experimental.pallas import tpu as pltpu

f32 = jnp.float32
bf16 = jnp.bfloat16

D_MODEL = 1024
DEPTH = 4
CHUNK = 64
D_PLE = 256
ML_HEADS, ML_DQK, ML_DV = 4, 64, 128
ML_W = ML_HEADS * ML_DV
RW_HEADS, RW_DH = 8, 64
RW_W = RW_HEADS * RW_DH
RW_W_RANK, RW_A_RANK, RW_G_RANK = 64, 64, 128
GD_HEADS, GD_DK, GD_DV = 4, 128, 128
GD_W = GD_HEADS * GD_DV
GD_CONV = 4
N_GROUPS, EXP_PER_GROUP = 4, 4
N_EXPERTS = N_GROUPS * EXP_PER_GROUP
D_EXPERT = 256
ML_COLS = 2 * ML_HEADS * ML_DQK + 2 * ML_W + 2 * ML_HEADS
RW_COLS = 3 * RW_W + RW_W_RANK + RW_A_RANK + RW_G_RANK
GD_CONV_CH = 2 * GD_HEADS * GD_DK + GD_W
GD_COLS = GD_CONV_CH + GD_W + 2 * GD_HEADS
GATE_COLS = 3 * D_MODEL
DEEPNORM_ALPHA = (2 * DEPTH) ** 0.25
LN_EPS = 1e-5
GN_EPS = 64e-5

ML_MAIN = 2 * ML_HEADS * ML_DQK + 2 * ML_W
SMALL_W = 256
GD_MAIN = GD_CONV_CH + GD_W
SEG_ML = 0
SEG_SMALL = ML_MAIN
SEG_RW = SEG_SMALL + SMALL_W
SEG_GD = 2 * GD_MAIN
SEG_GATE = SEG_GD + GD_MAIN
PROJ_COLS = SEG_GATE + GATE_COLS
assert SEG_RW == RW_COLS and SEG_GATE == 2 * GATE_COLS and SEG_GD >= SEG_RW + RW_COLS

VMEM_LIMIT = 56 * 1024 * 1024

_NN = (((1,), (0,)), ((), ()))
_NT = (((1,), (1,)), ((), ()))
_TN = (((0,), (0,)), ((), ()))


def _mm(a, b, dims=_NN):
    return lax.dot_general(a.astype(bf16), b.astype(bf16), dims, preferred_element_type=f32)


def _split2(x):
    hi = x.astype(bf16)
    return hi, (x - hi.astype(f32)).astype(bf16)


def _mm3(a, b, dims=_NN):
    ah, al = _split2(a)
    bh, bl = _split2(b)
    d = lambda x, y: lax.dot_general(x, y, dims, preferred_element_type=f32)
    return d(ah, bh) + (d(ah, bl) + d(al, bh))


def _split3(x):
    h1 = x.astype(bf16)
    r1 = x - h1.astype(f32)
    h2 = r1.astype(bf16)
    h3 = (r1 - h2.astype(f32)).astype(bf16)
    return h1, h2, h3


def _cumsum_rows(tri, x):
    d = lambda y: lax.dot_general(tri, y, _NN, preferred_element_type=f32)
    h1, h2, h3 = _split3(x)
    return d(h1) + (d(h2) + d(h3))


def _cumsum_lanes(x, tri_t):
    d = lambda y: lax.dot_general(y, tri_t, _NN, preferred_element_type=f32)
    h1, h2, h3 = _split3(x)
    return d(h1) + (d(h2) + d(h3))


def _softplus(x):
    return jnp.maximum(x, 0.0) + jnp.log1p(jnp.exp(-jnp.abs(x)))


def _log_sigmoid(x):
    return -_softplus(-x)


def _silu(x):
    return x * jax.nn.sigmoid(x)


def _iotas(n):
    r = lax.broadcasted_iota(jnp.int32, (n, n), 0)
    c = lax.broadcasted_iota(jnp.int32, (n, n), 1)
    return r, c


def _tri_inv(a, r, c):
    eye = (r == c).astype(f32)
    n = jnp.where((r >> 3) == (c >> 3), a, 0.0)
    n2 = _mm3(n, n)
    n4 = _mm3(n2, n2)
    x = _mm3(_mm3(eye - n, eye + n2), eye + n4)
    for sh in (3, 4, 5):
        m = jnp.where(((r >> (sh + 1)) == (c >> (sh + 1))) & ((r >> sh) != (c >> sh)), a, 0.0)
        x = x - _mm3(x, _mm3(m, x))
    return x


def _layer_norm(x, g, b):
    mu = jnp.mean(x, axis=-1, keepdims=True)
    xc = x - mu
    var = jnp.mean(xc * xc, axis=-1, keepdims=True)
    return xc * lax.rsqrt(var + LN_EPS) * g + b


def _params(n_axes):
    return pltpu.CompilerParams(dimension_semantics=("arbitrary",) * n_axes, vmem_limit_bytes=VMEM_LIMIT)


def _pick(n, cands):
    for t in cands:
        if n % t == 0:
            return t
    raise ValueError(f"no tile for {n} in {cands}")


def _inproj_body(x_ref, w_ref, o_ref, xb_s):
    @pl.when(pl.program_id(1) == 0)
    def _():
        xb_s[...] = x_ref[...].astype(bf16)

    o_ref[...] = jnp.dot(xb_s[...], w_ref[...], preferred_element_type=f32)


def _inproj(x, w):
    t, k = x.shape
    n = w.shape[1]
    tm = _pick(t, (1536, 768, 512, 384, 128))
    tn = 512
    return pl.pallas_call(
        _inproj_body,
        grid=(t // tm, n // tn),
        in_specs=[pl.BlockSpec((tm, k), lambda i, j: (i, 0)), pl.BlockSpec((k, tn), lambda i, j: (0, j))],
        out_specs=pl.BlockSpec((tm, tn), lambda i, j: (i, j)),
        out_shape=jax.ShapeDtypeStruct((t, n), f32),
        scratch_shapes=[pltpu.VMEM((tm, k), bf16)],
        compiler_params=_params(2),
        name="inproj",
    )(x, w)


def _mlstm_body(main_ref, sm_ref, brow_ref, bcol_ref, norm_ref, cn0_ref, m0_ref,
                h_ref, cn_out_ref, m_out_ref, cn_s, m_s, *, ncp):
    c = pl.program_id(0)
    L = CHUNK

    @pl.when(c == 0)
    def _():
        cn_s[...] = jnp.zeros_like(cn_s)
        m_s[...] = jnp.zeros_like(m_s)

    @pl.when(c >= ncp)
    def _():
        cn_s[...] = cn0_ref[0]
        m_s[...] = m0_ref[0]

    r, cc = _iotas(L)
    causal = r >= cc
    tri = causal.astype(bf16)
    tri_t = (r <= cc).astype(bf16)
    one_lane0 = (lax.broadcasted_iota(jnp.int32, (L, ML_DV), 1) == 0).astype(f32)

    sm = sm_ref[:, :128]
    g_c = sm + brow_ref[...]
    g_r = sm.T[:16, :] + bcol_ref[...]
    b_c = _cumsum_rows(tri, _log_sigmoid(g_c))
    b_r = _cumsum_lanes(_log_sigmoid(g_r), tri_t)

    for h in range(ML_HEADS):
        q = main_ref[:, h * ML_DQK:(h + 1) * ML_DQK] * (ML_DQK ** -0.5)
        k = main_ref[:, 256 + h * ML_DQK:256 + (h + 1) * ML_DQK]
        v = main_ref[:, 512 + h * ML_DV:512 + (h + 1) * ML_DV]
        o = main_ref[:, 1024 + h * ML_DV:1024 + (h + 1) * ML_DV]
        bc = b_c[:, 4 + h:5 + h]
        br = b_r[4 + h:5 + h, :]
        lic = g_c[:, h:h + 1]
        lir = g_r[h:h + 1, :]
        m_prev = m_s[h]
        cn = cn_s[h]

        d_intra = jnp.where(causal, bc - br + lir, -jnp.inf)
        d_inter = bc + m_prev
        m_t = jnp.maximum(d_inter, jnp.max(d_intra, axis=-1, keepdims=True))
        s = _mm(q, k, _NT) * jnp.exp(d_intra - m_t)
        w_inter = jnp.exp(d_inter - m_t)
        qcn = _mm(q, cn)
        num = w_inter * qcn[:, :ML_DV] + _mm(s, v)
        den = w_inter * qcn[:, ML_DV:ML_DV + 1] + jnp.sum(s, axis=-1, keepdims=True)
        hh = num / jnp.maximum(jnp.abs(den), jnp.exp(-m_t))

        b_last = bc[L - 1:L, :]
        d_last = b_last - bc + lic
        m_new = jnp.maximum(b_last + m_prev, jnp.max(d_last, axis=0, keepdims=True))
        w_last = jnp.exp(d_last - m_new)
        decay = jnp.exp(b_last + m_prev - m_new)
        vaug = jnp.concatenate([v, one_lane0], axis=-1) * w_last
        cn_new = decay * cn + _mm(k, vaug, _TN)
        cn_s[h] = cn_new
        m_s[h] = m_new
        cn_out_ref[0, h] = cn_new
        m_out_ref[0, h] = m_new

        hn = hh * lax.rsqrt(jnp.mean(hh * hh, axis=-1, keepdims=True) + 1e-6) * norm_ref[:, h * ML_DV:(h + 1) * ML_DV]
        h_ref[:, h * ML_DV:(h + 1) * ML_DV] = (hn * jax.nn.sigmoid(o)).astype(h_ref.dtype)


def _mlstm(cols, brow, bcol, norm, cn0, m0, ncp, ns):
    t = cols.shape[0]
    nst = ns + 1
    st_in = lambda c: (jnp.maximum(c - ncp, 0), 0, 0, 0)
    st_out = lambda c: (jnp.maximum(c - (ncp - 1), 0), 0, 0, 0)
    return pl.pallas_call(
        functools.partial(_mlstm_body, ncp=ncp),
        grid=(ncp + ns,),
        in_specs=[
            pl.BlockSpec((CHUNK, ML_MAIN), lambda c: (c, SEG_ML // ML_MAIN)),
            pl.BlockSpec((CHUNK, SMALL_W), lambda c: (c, SEG_SMALL // SMALL_W)),
            pl.BlockSpec((1, 128), lambda c: (0, 0)),
            pl.BlockSpec((16, 1), lambda c: (0, 0)),
            pl.BlockSpec((1, ML_W), lambda c: (0, 0)),
            pl.BlockSpec((1, ML_HEADS, ML_DQK, 2 * ML_DV), st_in),
            pl.BlockSpec((1, ML_HEADS, 1, 1), st_in),
        ],
        out_specs=[
            pl.BlockSpec((CHUNK, ML_W), lambda c: (c, 0)),
            pl.BlockSpec((1, ML_HEADS, ML_DQK, 2 * ML_DV), st_out),
            pl.BlockSpec((1, ML_HEADS, 1, 1), st_out),
        ],
        out_shape=[
            jax.ShapeDtypeStruct((t, ML_W), bf16),
            jax.ShapeDtypeStruct((nst, ML_HEADS, ML_DQK, 2 * ML_DV), f32),
            jax.ShapeDtypeStruct((nst, ML_HEADS, 1, 1), f32),
        ],
        scratch_shapes=[pltpu.VMEM((ML_HEADS, ML_DQK, 2 * ML_DV), f32), pltpu.VMEM((ML_HEADS, 1, 1), f32)],
        compiler_params=_params(1),
        name="mlstm",
    )(cols, cols, brow, bcol, norm, cn0, m0)


def _gdn_body(main_ref, sm_ref, convw_ref, arow_ref, acol_ref, dtrow_ref, dtcol_ref, norm_ref, conv0_ref, s0_ref,
              o_ref, s_out_ref, conv_out_ref, s_s, prev_s, *, ncp):
    c = pl.program_id(0)
    L = CHUNK

    @pl.when(c == 0)
    def _():
        s_s[...] = jnp.zeros_like(s_s)
        prev_s[...] = jnp.zeros_like(prev_s)

    @pl.when(c >= ncp)
    def _():
        s_s[...] = s0_ref[0]
        prev_s[...] = conv0_ref[0]

    r, cc = _iotas(L)
    incl = r >= cc
    strict = r > cc
    tri = incl.astype(bf16)
    tri_t = (r <= cc).astype(bf16)

    x = main_ref[:, :GD_CONV_CH]
    xfull = jnp.concatenate([prev_s[...], x], axis=0)
    conv = (convw_ref[3:4, :] * x + convw_ref[2:3, :] * xfull[7:7 + L]
            + convw_ref[1:2, :] * xfull[6:6 + L] + convw_ref[0:1, :] * xfull[5:5 + L])
    tail = x[L - 8:, :]
    prev_s[...] = tail
    conv_out_ref[0] = tail
    conv = _silu(conv)

    sm = sm_ref[:, :128]
    beta_all = jax.nn.sigmoid(sm)
    la_c = -jnp.exp(arow_ref[...]) * _softplus(sm + dtrow_ref[...])
    la_r = -jnp.exp(acol_ref[...]) * _softplus(sm.T[:16, :] + dtcol_ref[...])
    gc_c = _cumsum_rows(tri, la_c)
    gc_r = _cumsum_lanes(la_r, tri_t)

    for h in range(GD_HEADS):
        q = conv[:, h * GD_DK:(h + 1) * GD_DK]
        k = conv[:, 512 + h * GD_DK:512 + (h + 1) * GD_DK]
        v = conv[:, 1024 + h * GD_DV:1024 + (h + 1) * GD_DV]
        z = main_ref[:, GD_CONV_CH + h * GD_DV:GD_CONV_CH + (h + 1) * GD_DV]
        q = q * lax.rsqrt(jnp.sum(q * q, axis=-1, keepdims=True) + 1e-6) * (GD_DK ** -0.5)
        k = k * lax.rsqrt(jnp.sum(k * k, axis=-1, keepdims=True) + 1e-6)
        beta = beta_all[:, 8 + h:9 + h]
        gcc = gc_c[:, 12 + h:13 + h]
        gcr = gc_r[12 + h:13 + h, :]
        s = s_s[h]

        dec = jnp.exp(jnp.where(incl, gcc - gcr, -jnp.inf))
        a_mat = jnp.where(strict, beta * _mm(k, k, _NT) * dec, 0.0)
        rhs = jnp.concatenate([v * beta, k * (beta * jnp.exp(gcc))], axis=-1)
        sol = _mm3(_tri_inv(a_mat, r, cc), rhs)
        u = sol[:, :GD_DV] - _mm(sol[:, GD_DV:], s)
        qk = _mm(q, k, _NT) * dec
        o = _mm(q * jnp.exp(gcc), s) + _mm(qk, u)
        g_last = gcc[L - 1:L, :]
        s_new = jnp.exp(g_last) * s + _mm(k * jnp.exp(g_last - gcc), u, _TN)
        s_s[h] = s_new
        s_out_ref[0, h] = s_new

        on = o * lax.rsqrt(jnp.mean(o * o, axis=-1, keepdims=True) + 1e-6) * norm_ref[...]
        o_ref[:, h * GD_DV:(h + 1) * GD_DV] = (on * _silu(z)).astype(o_ref.dtype)


def _gdn(cols, convw, arow, acol, dtrow, dtcol, norm, conv0, s0, ncp, ns):
    t = cols.shape[0]
    nst = ns + 1
    st_in4 = lambda c: (jnp.maximum(c - ncp, 0), 0, 0, 0)
    st_in3 = lambda c: (jnp.maximum(c - ncp, 0), 0, 0)
    st_out4 = lambda c: (jnp.maximum(c - (ncp - 1), 0), 0, 0, 0)
    st_out3 = lambda c: (jnp.maximum(c - (ncp - 1), 0), 0, 0)
    const2 = lambda c: (0, 0)
    return pl.pallas_call(
        functools.partial(_gdn_body, ncp=ncp),
        grid=(ncp + ns,),
        in_specs=[
            pl.BlockSpec((CHUNK, GD_MAIN), lambda c: (c, SEG_GD // GD_MAIN)),
            pl.BlockSpec((CHUNK, SMALL_W), lambda c: (c, SEG_SMALL // SMALL_W)),
            pl.BlockSpec((8, GD_CONV_CH), const2),
            pl.BlockSpec((1, 128), const2),
            pl.BlockSpec((16, 1), const2),
            pl.BlockSpec((1, 128), const2),
            pl.BlockSpec((16, 1), const2),
            pl.BlockSpec((1, GD_DV), const2),
            pl.BlockSpec((1, 8, GD_CONV_CH), st_in3),
            pl.BlockSpec((1, GD_HEADS, GD_DK, GD_DV), st_in4),
        ],
        out_specs=[
            pl.BlockSpec((CHUNK, GD_W), lambda c: (c, 0)),
            pl.BlockSpec((1, GD_HEADS, GD_DK, GD_DV), st_out4),
            pl.BlockSpec((1, 8, GD_CONV_CH), st_out3),
        ],
        out_shape=[
            jax.ShapeDtypeStruct((t, GD_W), bf16),
            jax.ShapeDtypeStruct((nst, GD_HEADS, GD_DK, GD_DV), f32),
            jax.ShapeDtypeStruct((nst, 8, GD_CONV_CH), f32),
        ],
        scratch_shapes=[pltpu.VMEM((GD_HEADS, GD_DK, GD_DV), f32), pltpu.VMEM((8, GD_CONV_CH), f32)],
        compiler_params=_params(1),
        name="gdn",
    )(cols, cols, convw, arow, acol, dtrow, dtcol, norm, conv0, s0)


def _rwkv_body(rw_ref, mu_ref, w0_ref, w2_ref, a0_ref, a2_ref, g2_ref, kk_ref, ka_ref, rk_ref, gng_ref, gnb_ref,
               shift0_ref, s0_ref, y_ref, s_out_ref, shift_out_ref, s_s, prev_s, *, ncp):
    c = pl.program_id(0)
    L = CHUNK
    dh = RW_DH

    @pl.when(c == 0)
    def _():
        s_s[...] = jnp.zeros_like(s_s)
        prev_s[...] = jnp.zeros_like(prev_s)

    @pl.when(c >= ncp)
    def _():
        s_s[...] = s0_ref[0]
        prev_s[...] = shift0_ref[0]

    r_i, c_i = _iotas(L)
    incl = r_i >= c_i
    strict = r_i > c_i
    tri = incl.astype(bf16)

    x = rw_ref[...]
    xfull = jnp.concatenate([prev_s[...], x], axis=0)
    prev = xfull[7:7 + L]
    tail = x[L - 8:, :]
    prev_s[...] = tail
    shift_out_ref[0] = tail
    mixed = x + (prev - x) * mu_ref[...]
    r_all = mixed[:, 0:RW_W]
    k_all = mixed[:, RW_W:2 * RW_W]
    v_all = mixed[:, 2 * RW_W:3 * RW_W]
    w_lo = mixed[:, 3 * RW_W:3 * RW_W + RW_W_RANK]
    a_lo = mixed[:, 3 * RW_W + RW_W_RANK:3 * RW_W + RW_W_RANK + RW_A_RANK]
    g_lo = mixed[:, 3 * RW_W + RW_W_RANK + RW_A_RANK:]
    w_log = -_softplus(-(w0_ref[...] + _mm(jnp.tanh(w_lo), w2_ref[...]))) - 0.5
    lw = -jnp.exp(w_log)
    a_all = jax.nn.sigmoid(a0_ref[...] + _mm(a_lo, a2_ref[...]))
    g_all = _mm(jax.nn.sigmoid(g_lo), g2_ref[...])
    cum = _cumsum_rows(tri, lw)
    cum_ex = cum - lw

    for h in range(RW_HEADS):
        sl = slice(h * dh, (h + 1) * dh)
        rh, kh, vh, ah = r_all[:, sl], k_all[:, sl], v_all[:, sl], a_all[:, sl]
        kkh = kh * kk_ref[:, sl]
        kkh = kkh * lax.rsqrt(jnp.sum(kkh * kkh, axis=-1, keepdims=True) + 1e-6)
        kmod = kh * (1.0 + (ah - 1.0) * ka_ref[:, sl])
        cu = cum[:, sl]
        e_neg = jnp.exp(-cu)
        at = -kkh * jnp.exp(cum_ex[:, sl])
        rt = rh * jnp.exp(cu)
        kt = kmod * e_neg
        bt = kkh * ah * e_neg
        s0 = s_s[h]

        ar = jnp.concatenate([at, rt], axis=0)
        gmat = _mm3(ar, jnp.concatenate([kt, bt], axis=0), _NT)
        m_ak = jnp.where(strict, gmat[:L, :L], 0.0)
        n_ab = jnp.where(strict, gmat[:L, L:], 0.0)
        m_rk = jnp.where(incl, gmat[L:, :L], 0.0)
        n_rb = jnp.where(incl, gmat[L:, L:], 0.0)
        ars = _mm3(ar, s0, _NT)
        u = _mm3(_tri_inv(-n_ab, r_i, c_i), ars[:L] + _mm3(m_ak, vh))
        y = ars[L:] + _mm3(n_rb, u) + _mm3(m_rk, vh)
        s_new = (s0 + _mm3(u, bt, _TN) + _mm3(vh, kt, _TN)) * jnp.exp(cu[L - 1:L, :])
        s_s[h] = s_new
        s_out_ref[0, h] = s_new

        mu_y = jnp.mean(y, axis=-1, keepdims=True)
        yc = y - mu_y
        var_y = jnp.mean(yc * yc, axis=-1, keepdims=True)
        yn = yc * lax.rsqrt(var_y + GN_EPS) * gng_ref[:, sl] + gnb_ref[:, sl]
        yn = yn + jnp.sum(rh * kmod * rk_ref[:, sl], axis=-1, keepdims=True) * vh
        y_ref[:, sl] = (yn * g_all[:, sl]).astype(y_ref.dtype)


def _rwkv(cols, mu, w0, w2, a0, a2, g2, k_k, k_a, r_k, gn_g, gn_b, shift0, s0, ncp, ns):
    t = cols.shape[0]
    nst = ns + 1
    st_in4 = lambda c: (jnp.maximum(c - ncp, 0), 0, 0, 0)
    st_in3 = lambda c: (jnp.maximum(c - ncp, 0), 0, 0)
    st_out4 = lambda c: (jnp.maximum(c - (ncp - 1), 0), 0, 0, 0)
    st_out3 = lambda c: (jnp.maximum(c - (ncp - 1), 0), 0, 0)
    const2 = lambda c: (0, 0)
    row = lambda n: pl.BlockSpec((1, n), const2)
    return pl.pallas_call(
        functools.partial(_rwkv_body, ncp=ncp),
        grid=(ncp + ns,),
        in_specs=[
            pl.BlockSpec((CHUNK, RW_COLS), lambda c: (c, SEG_RW // RW_COLS)),
            row(RW_COLS), row(RW_W),
            pl.BlockSpec((RW_W_RANK, RW_W), const2),
            row(RW_W),
            pl.BlockSpec((RW_A_RANK, RW_W), const2),
            pl.BlockSpec((RW_G_RANK, RW_W), const2),
            row(RW_W), row(RW_W), row(RW_W), row(RW_W), row(RW_W),
            pl.BlockSpec((1, 8, RW_COLS), st_in3),
            pl.BlockSpec((1, RW_HEADS, RW_DH, RW_DH), st_in4),
        ],
        out_specs=[
            pl.BlockSpec((CHUNK, RW_W), lambda c: (c, 0)),
            pl.BlockSpec((1, RW_HEADS, RW_DH, RW_DH), st_out4),
            pl.BlockSpec((1, 8, RW_COLS), st_out3),
        ],
        out_shape=[
            jax.ShapeDtypeStruct((t, RW_W), bf16),
            jax.ShapeDtypeStruct((nst, RW_HEADS, RW_DH, RW_DH), f32),
            jax.ShapeDtypeStruct((nst, 8, RW_COLS), f32),
        ],
        scratch_shapes=[pltpu.VMEM((RW_HEADS, RW_DH, RW_DH), f32), pltpu.VMEM((8, RW_COLS), f32)],
        compiler_params=_params(1),
        name="rwkv",
    )(cols, mu, w0, w2, a0, a2, g2, k_k, k_a, r_k, gn_g, gn_b, shift0, s0)


def _merge_body(x_ref, g_ref, hml_ref, hrw_ref, hgd_ref, bg_ref, wml_ref, wrw_ref, wgd_ref, wout_ref,
                lng_ref, lnb_ref, o_ref):
    d = D_MODEL
    dot = lambda a, b: jnp.dot(a, b, preferred_element_type=f32)
    merged = (jax.nn.sigmoid(g_ref[:, 0:d] + bg_ref[:, 0:d]) * dot(hml_ref[...], wml_ref[...])
              + jax.nn.sigmoid(g_ref[:, d:2 * d] + bg_ref[:, d:2 * d]) * dot(hrw_ref[...], wrw_ref[...])
              + jax.nn.sigmoid(g_ref[:, 2 * d:] + bg_ref[:, 2 * d:]) * dot(hgd_ref[...], wgd_ref[...]))
    y = dot(merged.astype(bf16), wout_ref[...])
    o_ref[...] = _layer_norm(DEEPNORM_ALPHA * x_ref[...] + y, lng_ref[...], lnb_ref[...])


def _merge(x, cols, h_ml, h_rw, h_gd, b_gates, w_ml, w_rw, w_gd, w_out, ln_g, ln_b):
    t, d = x.shape
    tm = _pick(t, (512, 384, 128))
    tok = lambda n: pl.BlockSpec((tm, n), lambda i: (i, 0))
    full = lambda a: pl.BlockSpec(a.shape, lambda i: (0, 0))
    return pl.pallas_call(
        _merge_body,
        grid=(t // tm,),
        in_specs=[tok(d), pl.BlockSpec((tm, GATE_COLS), lambda i: (i, SEG_GATE // GATE_COLS)),
                  tok(ML_W), tok(RW_W), tok(GD_W),
                  full(b_gates), full(w_ml), full(w_rw), full(w_gd), full(w_out), full(ln_g), full(ln_b)],
        out_specs=tok(d),
        out_shape=jax.ShapeDtypeStruct((t, d), f32),
        compiler_params=_params(1),
        name="merge",
    )(x, cols, h_ml, h_rw, h_gd, b_gates, w_ml, w_rw, w_gd, w_out, ln_g, ln_b)


def _route(lt):
    grp = [lt[i:i + 1, :] for i in range(N_GROUPS)]
    gmax = functools.reduce(jnp.maximum, grp)
    gex = [jnp.exp(g - gmax) for g in grp]
    gsum = functools.reduce(lambda a, b: a + b, gex)
    gp = [e / gsum for e in gex]
    p_sel = functools.reduce(jnp.maximum, gp)
    taken = jnp.zeros_like(p_sel)
    sel = []
    for i in range(N_GROUPS):
        s_i = jnp.where(gp[i] == p_sel, 1.0, 0.0) * (1.0 - taken)
        taken = taken + s_i
        sel.append(s_i)
    e_in = []
    for j in range(EXP_PER_GROUP):
        acc = sel[0] * lt[8 + j:9 + j, :]
        for g in range(1, N_GROUPS):
            acc = acc + sel[g] * lt[8 + g * EXP_PER_GROUP + j:9 + g * EXP_PER_GROUP + j, :]
        e_in.append(acc)
    emax = functools.reduce(jnp.maximum, e_in)
    eex = [jnp.exp(e - emax) for e in e_in]
    esum = functools.reduce(lambda a, b: a + b, eex)
    ep = [e / esum for e in eex]
    top = []
    for j in range(EXP_PER_GROUP):
        rank = jnp.zeros_like(p_sel)
        for i in range(EXP_PER_GROUP):
            if i == j:
                continue
            ahead = (ep[i] > ep[j]) if i > j else (ep[i] >= ep[j])
            rank = rank + jnp.where(ahead, 1.0, 0.0)
        top.append(jnp.where(rank < 2.0, ep[j], 0.0))
    tsum = functools.reduce(lambda a, b: a + b, top)
    wts = [tp / tsum * p_sel for tp in top]
    return [sel[g] * wts[j] for g in range(N_GROUPS) for j in range(EXP_PER_GROUP)]


def _ffn_body(x_ref, pe_ref, wr_ref, br_ref, wg_ref, wu_ref, wd_ref, wpg_ref, bpg_ref, wp_ref, lng_ref, lnb_ref,
              o_ref, xb_s, gate_s, acc_s):
    e = pl.program_id(1)
    tm = x_ref.shape[0]

    @pl.when(e == 0)
    def _():
        x = x_ref[...]
        xb_s[...] = x.astype(bf16)
        logits = jnp.dot(x, wr_ref[...], precision=lax.Precision.HIGHEST, preferred_element_type=f32) + br_ref[...]
        rows = _route(logits.T)
        gt = jnp.concatenate(rows + [jnp.zeros((128 - N_EXPERTS, tm), f32)], axis=0)
        gate_s[...] = gt.T

    xb = xb_s[...]
    hg = jnp.dot(xb, wg_ref[0], preferred_element_type=f32)
    hu = jnp.dot(xb, wu_ref[0], preferred_element_type=f32)
    lane = lax.broadcasted_iota(jnp.int32, (tm, 128), 1)
    gcol = jnp.sum(jnp.where(lane == e, gate_s[...], 0.0), axis=-1, keepdims=True)
    hidden = (_silu(hg) * hu * gcol).astype(bf16)
    contrib = jnp.dot(hidden, wd_ref[0], preferred_element_type=f32)

    @pl.when(e == 0)
    def _():
        acc_s[...] = contrib

    @pl.when(e > 0)
    def _():
        acc_s[...] += contrib

    @pl.when(e == N_EXPERTS - 1)
    def _():
        ple = (jax.nn.sigmoid(jnp.dot(xb, wpg_ref[...], preferred_element_type=f32) + bpg_ref[...])
               * jnp.dot(pe_ref[...].astype(bf16), wp_ref[...], preferred_element_type=f32))
        o_ref[...] = _layer_norm(DEEPNORM_ALPHA * x_ref[...] + acc_s[...] + ple, lng_ref[...], lnb_ref[...])


def _ffn(x, pe, w_r, b_r, e_gate, e_up, e_down, w_pg, b_pg, w_p, ln_g, ln_b):
    t, d = x.shape
    tm = _pick(t, (768, 384, 128))
    tok = lambda n: pl.BlockSpec((tm, n), lambda i, e: (i, 0))
    full = lambda a: pl.BlockSpec(a.shape, lambda i, e: (0,) * a.ndim)
    return pl.pallas_call(
        _ffn_body,
        grid=(t // tm, N_EXPERTS),
        in_specs=[tok(d), tok(D_PLE), full(w_r), full(b_r),
                  pl.BlockSpec((1, d, D_EXPERT), lambda i, e: (e, 0, 0)),
                  pl.BlockSpec((1, d, D_EXPERT), lambda i, e: (e, 0, 0)),
                  pl.BlockSpec((1, D_EXPERT, d), lambda i, e: (e, 0, 0)),
                  full(w_pg), full(b_pg), full(w_p), full(ln_g), full(ln_b)],
        out_specs=tok(d),
        out_shape=jax.ShapeDtypeStruct((t, d), f32),
        scratch_shapes=[pltpu.VMEM((tm, d), bf16), pltpu.VMEM((tm, 128), f32), pltpu.VMEM((tm, d), f32)],
        compiler_params=_params(2),
        name="ffn",
    )(x, pe, w_r, b_r, e_gate, e_up, e_down, w_pg, b_pg, w_p, ln_g, ln_b)


def _lane_vec(n, off, vals):
    return jnp.zeros((1, n), f32).at[0, off:off + vals.shape[0]].set(vals)


def _layer(x, pe, st, lw, ncp, ns):
    ml_c, ml_n, ml_m, rw_s, rw_shift, gd_s, gd_conv = st
    w = lw['w_in']
    d = w.shape[0]
    z = lambda n: jnp.zeros((d, n), f32)
    w_proj = jnp.concatenate([
        w[:, 0:ML_MAIN],
        w[:, ML_MAIN:ML_COLS], w[:, ML_COLS + RW_COLS + GD_MAIN:ML_COLS + RW_COLS + GD_COLS], z(SMALL_W - 16),
        w[:, ML_COLS:ML_COLS + RW_COLS], z(SEG_GD - SEG_RW - RW_COLS),
        w[:, ML_COLS + RW_COLS:ML_COLS + RW_COLS + GD_MAIN],
        w[:, ML_COLS + RW_COLS + GD_COLS:],
    ], axis=1).astype(bf16)
    cols = _inproj(x, w_proj)

    gate_bias = jnp.concatenate([lw['ml_i_bias'], lw['ml_f_bias']])
    cn0 = jnp.concatenate([ml_c, ml_n[..., None], jnp.zeros(ml_c.shape[:-1] + (ML_DV - 1,), f32)], axis=-1)
    h_ml, cn1, m1 = _mlstm(cols, _lane_vec(128, 0, gate_bias), _lane_vec(16, 0, gate_bias).T,
                           lw['ml_norm'][None, :], cn0, ml_m[..., None, None], ncp, ns)
    row = lambda a: a.reshape(1, -1)
    shift0 = jnp.concatenate([jnp.zeros((ns, 7, RW_COLS), f32), rw_shift], axis=1)
    h_rw, rws1, shift1 = _rwkv(cols, row(lw['rw_mu']), row(lw['rw_w0']), lw['rw_w2'].astype(bf16), row(lw['rw_a0']),
                               lw['rw_a2'].astype(bf16), lw['rw_g2'].astype(bf16), row(lw['rw_k_k']), row(lw['rw_k_a']),
                               row(lw['rw_r_k']), row(lw['rw_gn_g']), row(lw['rw_gn_b']), shift0, rw_s, ncp, ns)
    conv0 = jnp.concatenate([jnp.zeros((ns, 8 - (GD_CONV - 1), GD_CONV_CH), f32), gd_conv], axis=1)
    convw = jnp.concatenate([lw['gd_conv_w'], jnp.zeros((8 - GD_CONV, GD_CONV_CH), f32)], axis=0)
    h_gd, gds1, conv1 = _gdn(cols, convw, _lane_vec(128, 12, lw['gd_a_log']), _lane_vec(16, 12, lw['gd_a_log']).T,
                             _lane_vec(128, 12, lw['gd_dt_bias']), _lane_vec(16, 12, lw['gd_dt_bias']).T,
                             lw['gd_norm'][None, :], conv0, gd_s, ncp, ns)

    x = _merge(x, cols, h_ml, h_rw, h_gd, lw['b_gates'].reshape(1, -1), lw['w_br_ml'].astype(bf16),
               lw['w_br_rw'].astype(bf16), lw['w_br_gd'].astype(bf16), lw['w_out'].astype(bf16),
               row(lw['ln1_g']), row(lw['ln1_b']))

    w_r = jnp.concatenate([lw['w_rg'], z(8 - N_GROUPS), lw['w_re'], z(128 - 8 - N_EXPERTS)], axis=1)
    b_r = jnp.concatenate([lw['b_rg'], jnp.zeros((8 - N_GROUPS,), f32), lw['b_re'],
                           jnp.zeros((128 - 8 - N_EXPERTS,), f32)])[None, :]
    x = _ffn(x, pe, w_r, b_r, lw['e_gate'].astype(bf16), lw['e_up'].astype(bf16), lw['e_down'].astype(bf16),
             lw['ple_gate_w'].astype(bf16), row(lw['ple_gate_b']), lw['ple_w'].astype(bf16),
             row(lw['ln2_g']), row(lw['ln2_b']))

    new_st = (cn1[..., :ML_DV], cn1[..., ML_DV], m1[..., 0, 0], rws1, shift1[:, 7:8, :], gds1,
              conv1[:, 8 - (GD_CONV - 1):, :])
    return x, new_st


_WEIGHT_NAMES = ('w_in', 'ml_i_bias', 'ml_f_bias', 'ml_norm', 'rw_mu', 'rw_w0', 'rw_w2', 'rw_a0', 'rw_a2', 'rw_g2',
                 'rw_k_k', 'rw_k_a', 'rw_r_k', 'rw_gn_g', 'rw_gn_b', 'gd_conv_w', 'gd_a_log', 'gd_dt_bias', 'gd_norm',
                 'w_br_ml', 'w_br_rw', 'w_br_gd', 'b_gates', 'w_out', 'ln1_g', 'ln1_b', 'w_rg', 'b_rg', 'w_re', 'b_re',
                 'e_gate', 'e_up', 'e_down', 'ple_w', 'ple_gate_w', 'ple_gate_b', 'ln2_g', 'ln2_b')


def _trunk(x_prompt, x_sample, p_prompt, p_sample, states, weights):
    bp, seq, d = x_prompt.shape
    bs, dseq, _ = x_sample.shape
    assert bp == 1 and seq % CHUNK == 0 and dseq == CHUNK
    ncp, ns = seq // CHUNK, bs
    depth = p_prompt.shape[0]
    x = jnp.concatenate([x_prompt.reshape(seq, d), x_sample.reshape(bs * dseq, d)], axis=0)
    new_states = []
    for l in range(depth):
        pe = jnp.concatenate([p_prompt[l].reshape(seq, -1), p_sample[l].reshape(bs * dseq, -1)], axis=0)
        lw = {k: v[l] for k, v in weights.items()}
        x, st = _layer(x, pe, tuple(s[l] for s in states), lw, ncp, ns)
        new_states.append(st)
    y_p = x[:seq].reshape(bp, seq, d)
    y_s = x[seq:].reshape(bs, dseq, d)
    stacked = [jnp.stack([st[j] for st in new_states]) for j in range(7)]
    outs_p = [s[:, :1] for s in stacked]
    outs_s = [s[:, 1:] for s in stacked]
    return (y_p, y_s, *outs_p, *outs_s)


def kernel(x_prompt, x_sample, p_prompt, p_sample, state_mlstm_c, state_mlstm_n, state_mlstm_m, state_rwkv_s, state_rwkv_shift, state_gdn_s, state_gdn_conv, w_in, ml_i_bias, ml_f_bias, ml_norm, rw_mu, rw_w0, rw_w2, rw_a0, rw_a2, rw_g2, rw_k_k, rw_k_a, rw_r_k, rw_gn_g, rw_gn_b, gd_conv_w, gd_a_log, gd_dt_bias, gd_norm, w_br_ml, w_br_rw, w_br_gd, b_gates, w_out, ln1_g, ln1_b, w_rg, b_rg, w_re, b_re, e_gate, e_up, e_down, ple_w, ple_gate_w, ple_gate_b, ln2_g, ln2_b):
    states = (state_mlstm_c, state_mlstm_n, state_mlstm_m, state_rwkv_s, state_rwkv_shift, state_gdn_s, state_gdn_conv)
    wvals = (w_in, ml_i_bias, ml_f_bias, ml_norm, rw_mu, rw_w0, rw_w2, rw_a0, rw_a2, rw_g2, rw_k_k, rw_k_a, rw_r_k,
             rw_gn_g, rw_gn_b, gd_conv_w, gd_a_log, gd_dt_bias, gd_norm, w_br_ml, w_br_rw, w_br_gd, b_gates, w_out,
             ln1_g, ln1_b, w_rg, b_rg, w_re, b_re, e_gate, e_up, e_down, ple_w, ple_gate_w, ple_gate_b, ln2_g, ln2_b)
    return _trunk(x_prompt, x_sample, p_prompt, p_sample, states, dict(zip(_WEIGHT_NAMES, wvals)))
```

```python
import functools

import jax
import jax.numpy as jnp
from jax import lax
from jax.experimental import pallas as pl
from jax.experimental.pallas import tpu as pltpu

f32 = jnp.float32
bf16 = jnp.bfloat16

D_MODEL = 1024
DEPTH = 4
CHUNK = 64
D_PLE = 256
ML_HEADS, ML_DQK, ML_DV = 4, 64, 128
ML_W = ML_HEADS * ML_DV
RW_HEADS, RW_DH = 8, 64
RW_W = RW_HEADS * RW_DH
RW_W_RANK, RW_A_RANK, RW_G_RANK = 64, 64, 128
GD_HEADS, GD_DK, GD_DV = 4, 128, 128
GD_W = GD_HEADS * GD_DV
GD_CONV = 4
N_GROUPS, EXP_PER_GROUP = 4, 4
N_EXPERTS = N_GROUPS * EXP_PER_GROUP
D_EXPERT = 256
ML_COLS = 2 * ML_HEADS * ML_DQK + 2 * ML_W + 2 * ML_HEADS
RW_COLS = 3 * RW_W + RW_W_RANK + RW_A_RANK + RW_G_RANK
GD_CONV_CH = 2 * GD_HEADS * GD_DK + GD_W
GD_COLS = GD_CONV_CH + GD_W + 2 * GD_HEADS
GATE_COLS = 3 * D_MODEL
DEEPNORM_ALPHA = (2 * DEPTH) ** 0.25
LN_EPS = 1e-5
GN_EPS = 64e-5

ML_MAIN = 2 * ML_HEADS * ML_DQK + 2 * ML_W
SMALL_W = 256
GD_MAIN = GD_CONV_CH + GD_W
SEG_ML = 0
SEG_SMALL = ML_MAIN
SEG_RW = SEG_SMALL + SMALL_W
SEG_GD = 2 * GD_MAIN
SEG_GATE = SEG_GD + GD_MAIN
PROJ_COLS = SEG_GATE + GATE_COLS
assert SEG_RW == RW_COLS and SEG_GATE == 2 * GATE_COLS and SEG_GD >= SEG_RW + RW_COLS

VMEM_LIMIT = 56 * 1024 * 1024

_NN = (((1,), (0,)), ((), ()))
_NT = (((1,), (1,)), ((), ()))
_TN = (((0,), (0,)), ((), ()))


def _mm(a, b, dims=_NN):
    return lax.dot_general(a.astype(bf16), b.astype(bf16), dims, preferred_element_type=f32)


def _split3(x):
    h1 = x.astype(bf16)
    r1 = x - h1.astype(f32)
    h2 = r1.astype(bf16)
    h3 = (r1 - h2.astype(f32)).astype(bf16)
    return h1, h2, h3


def _cumsum_rows(tri, x):
    d = lambda y: lax.dot_general(tri, y, _NN, preferred_element_type=f32)
    h1, h2, h3 = _split3(x)
    return d(h1) + (d(h2) + d(h3))


def _cumsum_lanes(x, tri_t):
    d = lambda y: lax.dot_general(y, tri_t, _NN, preferred_element_type=f32)
    h1, h2, h3 = _split3(x)
    return d(h1) + (d(h2) + d(h3))


def _softplus(x):
    return jnp.maximum(x, 0.0) + jnp.log1p(jnp.exp(-jnp.abs(x)))


def _log_sigmoid(x):
    return -_softplus(-x)


def _silu(x):
    return x * jax.nn.sigmoid(x)


def _iotas(n):
    r = lax.broadcasted_iota(jnp.int32, (n, n), 0)
    c = lax.broadcasted_iota(jnp.int32, (n, n), 1)
    return r, c


def _tri_inv_many(mats, r, c):
    d = lambda x, y: lax.dot_general(x, y, _NN, preferred_element_type=f32)
    eye = (r == c).astype(f32)
    blk8 = (r >> 3) == (c >> 3)
    ns = [jnp.where(blk8, a, 0.0) for a in mats]
    nb = [n.astype(bf16) for n in ns]
    n2 = [d(p, p) for p in nb]
    n2b = [x.astype(bf16) for x in n2]
    n3 = [d(p, q) for p, q in zip(nb, n2b)]
    n4 = [d(q, q) for q in n2b]
    ps = [eye - n + m2 - m3 for n, m2, m3 in zip(ns, n2, n3)]
    xs = [p + d(p.astype(bf16), m4.astype(bf16)) for p, m4 in zip(ps, n4)]
    for sh in (3, 4, 5):
        off = ((r >> (sh + 1)) == (c >> (sh + 1))) & ((r >> sh) != (c >> sh))
        xb = [x.astype(bf16) for x in xs]
        ts = [d(jnp.where(off, a, 0.0).astype(bf16), q) for a, q in zip(mats, xb)]
        xs = [x - d(q, t.astype(bf16)) for x, q, t in zip(xs, xb, ts)]
    return xs


def _layer_norm(x, g, b):
    mu = jnp.mean(x, axis=-1, keepdims=True)
    xc = x - mu
    var = jnp.mean(xc * xc, axis=-1, keepdims=True)
    return xc * lax.rsqrt(var + LN_EPS) * g + b


def _params(n_axes):
    return pltpu.CompilerParams(dimension_semantics=("arbitrary",) * n_axes, vmem_limit_bytes=VMEM_LIMIT)


def _pick(n, cands):
    for t in cands:
        if n % t == 0:
            return t
    raise ValueError(f"no tile for {n} in {cands}")


def _inproj_body(x_ref, w_ref, o_ref, xb_s):
    @pl.when(pl.program_id(1) == 0)
    def _():
        xb_s[...] = x_ref[...].astype(bf16)

    o_ref[...] = jnp.dot(xb_s[...], w_ref[...], preferred_element_type=f32)


def _inproj(x, w):
    t, k = x.shape
    n = w.shape[1]
    tm = _pick(t, (1536, 768, 512, 384, 128))
    tn = 512
    return pl.pallas_call(
        _inproj_body,
        grid=(t // tm, n // tn),
        in_specs=[pl.BlockSpec((tm, k), lambda i, j: (i, 0)), pl.BlockSpec((k, tn), lambda i, j: (0, j))],
        out_specs=pl.BlockSpec((tm, tn), lambda i, j: (i, j)),
        out_shape=jax.ShapeDtypeStruct((t, n), f32),
        scratch_shapes=[pltpu.VMEM((tm, k), bf16)],
        compiler_params=_params(2),
        name="inproj",
    )(x, w)


def _mlstm_body(main_ref, sm_ref, brow_ref, bcol_ref, norm_ref, cn0_ref, m0_ref,
                h_ref, cn_out_ref, m_out_ref, cn_s, m_s, *, ncp):
    c = pl.program_id(0)
    L = CHUNK

    @pl.when(c == 0)
    def _():
        cn_s[...] = jnp.zeros_like(cn_s)
        m_s[...] = jnp.zeros_like(m_s)

    @pl.when(c >= ncp)
    def _():
        cn_s[...] = cn0_ref[0]
        m_s[...] = m0_ref[0]

    r, cc = _iotas(L)
    causal = r >= cc
    tri = causal.astype(bf16)
    tri_t = (r <= cc).astype(bf16)
    one_lane0 = (lax.broadcasted_iota(jnp.int32, (L, ML_DV), 1) == 0).astype(f32)

    sm = sm_ref[:, :128]
    g_c = sm + brow_ref[...]
    g_r = sm.T[:16, :] + bcol_ref[...]
    b_c = _cumsum_rows(tri, _log_sigmoid(g_c))
    b_r = _cumsum_lanes(_log_sigmoid(g_r), tri_t)

    hs = range(ML_HEADS)
    m_prev = [m_s[h] for h in hs]
    cn = [cn_s[h] for h in hs]
    qb = [(main_ref[:, h * ML_DQK:(h + 1) * ML_DQK] * (ML_DQK ** -0.5)).astype(bf16) for h in hs]
    kb = [main_ref[:, 256 + h * ML_DQK:256 + (h + 1) * ML_DQK].astype(bf16) for h in hs]
    v = [main_ref[:, 512 + h * ML_DV:512 + (h + 1) * ML_DV] for h in hs]
    bc = [b_c[:, 4 + h:5 + h] for h in hs]
    qk = [lax.dot_general(qb[h], kb[h], _NT, preferred_element_type=f32) for h in hs]
    qcn = [lax.dot_general(qb[h], cn[h].astype(bf16), _NN, preferred_element_type=f32) for h in hs]
    s, m_t, w_inter = [], [], []
    for h in hs:
        d_intra = jnp.where(causal, bc[h] - b_r[4 + h:5 + h, :] + g_r[h:h + 1, :], -jnp.inf)
        d_inter = bc[h] + m_prev[h]
        m_t.append(jnp.maximum(d_inter, jnp.max(d_intra, axis=-1, keepdims=True)))
        s.append(qk[h] * jnp.exp(d_intra - m_t[h]))
        w_inter.append(jnp.exp(d_inter - m_t[h]))
    sv = [_mm(s[h], v[h]) for h in hs]
    cn_new, m_new = [], []
    for h in hs:
        b_last = bc[h][L - 1:L, :]
        d_last = b_last - bc[h] + g_c[:, h:h + 1]
        m_new.append(jnp.maximum(b_last + m_prev[h], jnp.max(d_last, axis=0, keepdims=True)))
        w_last = jnp.exp(d_last - m_new[h])
        decay = jnp.exp(b_last + m_prev[h] - m_new[h])
        vaug = (jnp.concatenate([v[h], one_lane0], axis=-1) * w_last).astype(bf16)
        cn_new.append(decay * cn[h] + lax.dot_general(kb[h], vaug, _TN, preferred_element_type=f32))
    outs = []
    for h in hs:
        num = w_inter[h] * qcn[h][:, :ML_DV] + sv[h]
        den = w_inter[h] * qcn[h][:, ML_DV:ML_DV + 1] + jnp.sum(s[h], axis=-1, keepdims=True)
        hh = num / jnp.maximum(jnp.abs(den), jnp.exp(-m_t[h]))
        hn = hh * lax.rsqrt(jnp.mean(hh * hh, axis=-1, keepdims=True) + 1e-6) * norm_ref[:, h * ML_DV:(h + 1) * ML_DV]
        o = main_ref[:, 1024 + h * ML_DV:1024 + (h + 1) * ML_DV]
        outs.append((hn * jax.nn.sigmoid(o)).astype(h_ref.dtype))

    for h in hs:
        cn_s[h] = cn_new[h]
        m_s[h] = m_new[h]
        cn_out_ref[0, h] = cn_new[h]
        m_out_ref[0, h] = m_new[h]
        h_ref[:, h * ML_DV:(h + 1) * ML_DV] = outs[h]


def _mlstm(cols, brow, bcol, norm, cn0, m0, ncp, ns):
    t = cols.shape[0]
    nst = ns + 1
    st_in = lambda c: (jnp.maximum(c - ncp, 0), 0, 0, 0)
    st_out = lambda c: (jnp.maximum(c - (ncp - 1), 0), 0, 0, 0)
    return pl.pallas_call(
        functools.partial(_mlstm_body, ncp=ncp),
        grid=(ncp + ns,),
        in_specs=[
            pl.BlockSpec((CHUNK, ML_MAIN), lambda c: (c, SEG_ML // ML_MAIN)),
            pl.BlockSpec((CHUNK, SMALL_W), lambda c: (c, SEG_SMALL // SMALL_W)),
            pl.BlockSpec((1, 128), lambda c: (0, 0)),
            pl.BlockSpec((16, 1), lambda c: (0, 0)),
            pl.BlockSpec((1, ML_W), lambda c: (0, 0)),
            pl.BlockSpec((1, ML_HEADS, ML_DQK, 2 * ML_DV), st_in),
            pl.BlockSpec((1, ML_HEADS, 1, 1), st_in),
        ],
        out_specs=[
            pl.BlockSpec((CHUNK, ML_W), lambda c: (c, 0)),
            pl.BlockSpec((1, ML_HEADS, ML_DQK, 2 * ML_DV), st_out),
            pl.BlockSpec((1, ML_HEADS, 1, 1), st_out),
        ],
        out_shape=[
            jax.ShapeDtypeStruct((t, ML_W), bf16),
            jax.ShapeDtypeStruct((nst, ML_HEADS, ML_DQK, 2 * ML_DV), f32),
            jax.ShapeDtypeStruct((nst, ML_HEADS, 1, 1), f32),
        ],
        scratch_shapes=[pltpu.VMEM((ML_HEADS, ML_DQK, 2 * ML_DV), f32), pltpu.VMEM((ML_HEADS, 1, 1), f32)],
        compiler_params=_params(1),
        name="mlstm",
    )(cols, cols, brow, bcol, norm, cn0, m0)


def _gdn_body(main_ref, sm_ref, convw_ref, arow_ref, acol_ref, dtrow_ref, dtcol_ref, norm_ref, conv0_ref, s0_ref,
              o_ref, s_out_ref, conv_out_ref, s_s, prev_s, *, ncp):
    c = pl.program_id(0)
    L = CHUNK

    @pl.when(c == 0)
    def _():
        s_s[...] = jnp.zeros_like(s_s)
        prev_s[...] = jnp.zeros_like(prev_s)

    @pl.when(c >= ncp)
    def _():
        s_s[...] = s0_ref[0]
        prev_s[...] = conv0_ref[0]

    r, cc = _iotas(L)
    incl = r >= cc
    strict = r > cc
    tri = incl.astype(bf16)
    tri_t = (r <= cc).astype(bf16)

    hs = range(GD_HEADS)
    x = main_ref[:, :GD_CONV_CH]
    s0 = [s_s[h] for h in hs]
    xfull = jnp.concatenate([prev_s[...], x], axis=0)
    conv = (convw_ref[3:4, :] * x + convw_ref[2:3, :] * xfull[7:7 + L]
            + convw_ref[1:2, :] * xfull[6:6 + L] + convw_ref[0:1, :] * xfull[5:5 + L])
    tail = x[L - 8:, :]
    conv = _silu(conv)

    sm = sm_ref[:, :128]
    beta_all = jax.nn.sigmoid(sm)
    la_c = -jnp.exp(arow_ref[...]) * _softplus(sm + dtrow_ref[...])
    la_r = -jnp.exp(acol_ref[...]) * _softplus(sm.T[:16, :] + dtcol_ref[...])
    gc_c = _cumsum_rows(tri, la_c)
    gc_r = _cumsum_lanes(la_r, tri_t)

    q, k, v, beta, gcc, dec = [], [], [], [], [], []
    for h in hs:
        qh = conv[:, h * GD_DK:(h + 1) * GD_DK]
        kh = conv[:, 512 + h * GD_DK:512 + (h + 1) * GD_DK]
        q.append(qh * lax.rsqrt(jnp.sum(qh * qh, axis=-1, keepdims=True) + 1e-6) * (GD_DK ** -0.5))
        k.append(kh * lax.rsqrt(jnp.sum(kh * kh, axis=-1, keepdims=True) + 1e-6))
        v.append(conv[:, 1024 + h * GD_DV:1024 + (h + 1) * GD_DV])
        beta.append(beta_all[:, 8 + h:9 + h])
        gcc.append(gc_c[:, 12 + h:13 + h])
        dec.append(jnp.exp(jnp.where(incl, gcc[h] - gc_r[12 + h:13 + h, :], -jnp.inf)))
    kb = [x_.astype(bf16) for x_ in k]
    kk = [lax.dot_general(kb[h], kb[h], _NT, preferred_element_type=f32) for h in hs]
    qk = [lax.dot_general(q[h].astype(bf16), kb[h], _NT, preferred_element_type=f32) * dec[h] for h in hs]
    qs = [_mm(q[h] * jnp.exp(gcc[h]), s0[h]) for h in hs]
    a_mat = [jnp.where(strict, beta[h] * kk[h] * dec[h], 0.0) for h in hs]
    rhs = [jnp.concatenate([v[h] * beta[h], k[h] * (beta[h] * jnp.exp(gcc[h]))], axis=-1) for h in hs]
    tinv = _tri_inv_many(a_mat, r, cc)
    sol = [_mm(tinv[h], rhs[h]) for h in hs]
    u = [sol[h][:, :GD_DV] - _mm(sol[h][:, GD_DV:], s0[h]) for h in hs]
    ub = [x_.astype(bf16) for x_ in u]
    o = [qs[h] + lax.dot_general(qk[h].astype(bf16), ub[h], _NN, preferred_element_type=f32) for h in hs]
    s_new = []
    for h in hs:
        g_last = gcc[h][L - 1:L, :]
        kd = (k[h] * jnp.exp(g_last - gcc[h])).astype(bf16)
        s_new.append(jnp.exp(g_last) * s0[h] + lax.dot_general(kd, ub[h], _TN, preferred_element_type=f32))
    outs = []
    for h in hs:
        z = main_ref[:, GD_CONV_CH + h * GD_DV:GD_CONV_CH + (h + 1) * GD_DV]
        on = o[h] * lax.rsqrt(jnp.mean(o[h] * o[h], axis=-1, keepdims=True) + 1e-6) * norm_ref[...]
        outs.append((on * _silu(z)).astype(o_ref.dtype))

    prev_s[...] = tail
    conv_out_ref[0] = tail
    for h in hs:
        s_s[h] = s_new[h]
        s_out_ref[0, h] = s_new[h]
        o_ref[:, h * GD_DV:(h + 1) * GD_DV] = outs[h]


def _gdn(cols, convw, arow, acol, dtrow, dtcol, norm, conv0, s0, ncp, ns):
    t = cols.shape[0]
    nst = ns + 1
    st_in4 = lambda c: (jnp.maximum(c - ncp, 0), 0, 0, 0)
    st_in3 = lambda c: (jnp.maximum(c - ncp, 0), 0, 0)
    st_out4 = lambda c: (jnp.maximum(c - (ncp - 1), 0), 0, 0, 0)
    st_out3 = lambda c: (jnp.maximum(c - (ncp - 1), 0), 0, 0)
    const2 = lambda c: (0, 0)
    return pl.pallas_call(
        functools.partial(_gdn_body, ncp=ncp),
        grid=(ncp + ns,),
        in_specs=[
            pl.BlockSpec((CHUNK, GD_MAIN), lambda c: (c, SEG_GD // GD_MAIN)),
            pl.BlockSpec((CHUNK, SMALL_W), lambda c: (c, SEG_SMALL // SMALL_W)),
            pl.BlockSpec((8, GD_CONV_CH), const2),
            pl.BlockSpec((1, 128), const2),
            pl.BlockSpec((16, 1), const2),
            pl.BlockSpec((1, 128), const2),
            pl.BlockSpec((16, 1), const2),
            pl.BlockSpec((1, GD_DV), const2),
            pl.BlockSpec((1, 8, GD_CONV_CH), st_in3),
            pl.BlockSpec((1, GD_HEADS, GD_DK, GD_DV), st_in4),
        ],
        out_specs=[
            pl.BlockSpec((CHUNK, GD_W), lambda c: (c, 0)),
            pl.BlockSpec((1, GD_HEADS, GD_DK, GD_DV), st_out4),
            pl.BlockSpec((1, 8, GD_CONV_CH), st_out3),
        ],
        out_shape=[
            jax.ShapeDtypeStruct((t, GD_W), bf16),
            jax.ShapeDtypeStruct((nst, GD_HEADS, GD_DK, GD_DV), f32),
            jax.ShapeDtypeStruct((nst, 8, GD_CONV_CH), f32),
        ],
        scratch_shapes=[pltpu.VMEM((GD_HEADS, GD_DK, GD_DV), f32), pltpu.VMEM((8, GD_CONV_CH), f32)],
        compiler_params=_params(1),
        name="gdn",
    )(cols, cols, convw, arow, acol, dtrow, dtcol, norm, conv0, s0)


def _rwkv_body(rw_ref, mu_ref, w0_ref, w2_ref, a0_ref, a2_ref, g2_ref, kk_ref, ka_ref, rk_ref, gng_ref, gnb_ref,
               shift0_ref, s0_ref, y_ref, s_out_ref, shift_out_ref, s_s, prev_s, *, ncp):
    c = pl.program_id(0)
    L = CHUNK
    dh = RW_DH

    @pl.when(c == 0)
    def _():
        s_s[...] = jnp.zeros_like(s_s)
        prev_s[...] = jnp.zeros_like(prev_s)

    @pl.when(c >= ncp)
    def _():
        s_s[...] = s0_ref[0]
        prev_s[...] = shift0_ref[0]

    r_i, c_i = _iotas(L)
    incl = r_i >= c_i
    strict = r_i > c_i
    tri = incl.astype(bf16)

    x = rw_ref[...]
    xfull = jnp.concatenate([prev_s[...], x], axis=0)
    prev = xfull[7:7 + L]
    tail = x[L - 8:, :]
    mixed = x + (prev - x) * mu_ref[...]
    r_all = mixed[:, 0:RW_W]
    k_all = mixed[:, RW_W:2 * RW_W]
    v_all = mixed[:, 2 * RW_W:3 * RW_W]
    w_lo = mixed[:, 3 * RW_W:3 * RW_W + RW_W_RANK]
    a_lo = mixed[:, 3 * RW_W + RW_W_RANK:3 * RW_W + RW_W_RANK + RW_A_RANK]
    g_lo = mixed[:, 3 * RW_W + RW_W_RANK + RW_A_RANK:]
    w_log = -_softplus(-(w0_ref[...] + _mm(jnp.tanh(w_lo), w2_ref[...]))) - 0.5
    lw = -jnp.exp(w_log)
    a_all = jax.nn.sigmoid(a0_ref[...] + _mm(a_lo, a2_ref[...]))
    g_all = _mm(jax.nn.sigmoid(g_lo), g2_ref[...])
    cum = _cumsum_rows(tri, lw)
    cum_ex = cum - lw

    kk_all = k_all * kk_ref[...]
    kmod_all = k_all * (1.0 + (a_all - 1.0) * ka_ref[...])
    e_neg = jnp.exp(-cum)
    rt_all = r_all * jnp.exp(cum)
    kt_all = kmod_all * e_neg
    ex_all = jnp.exp(cum_ex)
    ae_all = a_all * e_neg
    bonus_all = r_all * kmod_all * rk_ref[...]
    p_last = jnp.exp(cum[L - 1:L, :])

    hs = range(RW_HEADS)
    sls = [slice(h * dh, (h + 1) * dh) for h in hs]
    s0 = [s_s[h] for h in hs]
    vh = [v_all[:, sl] for sl in sls]
    ar, kb, bt = [], [], []
    for h in hs:
        sl = sls[h]
        kkh = kk_all[:, sl]
        kkh = kkh * lax.rsqrt(jnp.sum(kkh * kkh, axis=-1, keepdims=True) + 1e-6)
        at = -kkh * ex_all[:, sl]
        bt.append(kkh * ae_all[:, sl])
        ar.append(jnp.concatenate([at, rt_all[:, sl]], axis=0))
        kb.append(jnp.concatenate([kt_all[:, sl], bt[h]], axis=0))
    arb = [x_.astype(bf16) for x_ in ar]
    gmat = [lax.dot_general(arb[h], kb[h].astype(bf16), _NT, preferred_element_type=f32)
            for h in hs]
    ars = [lax.dot_general(arb[h], s0[h].astype(bf16), _NT, preferred_element_type=f32)
           for h in hs]
    mk = [jnp.concatenate([jnp.where(strict, gmat[h][:L, :L], 0.0), jnp.where(incl, gmat[h][L:, :L], 0.0)], axis=0)
          for h in hs]
    n_ab = [jnp.where(strict, -gmat[h][:L, L:], 0.0) for h in hs]
    n_rb = [jnp.where(incl, gmat[h][L:, L:], 0.0) for h in hs]
    mkv = [_mm(mk[h], vh[h]) for h in hs]
    tinv = _tri_inv_many(n_ab, r_i, c_i)
    u = [_mm(tinv[h], ars[h][:L] + mkv[h][:L]) for h in hs]
    y = [ars[h][L:] + mkv[h][L:] + _mm(n_rb[h], u[h]) for h in hs]
    bk = [jnp.concatenate([bt[h], kt_all[:, sls[h]]], axis=0) for h in hs]
    s_new = [(s0[h] + _mm(jnp.concatenate([u[h], vh[h]], axis=0), bk[h], _TN)) * p_last[:, sls[h]]
             for h in hs]
    outs = []
    for h in hs:
        sl = sls[h]
        mu_y = jnp.mean(y[h], axis=-1, keepdims=True)
        yc = y[h] - mu_y
        var_y = jnp.mean(yc * yc, axis=-1, keepdims=True)
        yn = yc * lax.rsqrt(var_y + GN_EPS) * gng_ref[:, sl] + gnb_ref[:, sl]
        yn = yn + jnp.sum(bonus_all[:, sl], axis=-1, keepdims=True) * vh[h]
        outs.append(yn * g_all[:, sl])

    prev_s[...] = tail
    shift_out_ref[0] = tail
    for h in hs:
        s_s[h] = s_new[h]
        s_out_ref[0, h] = s_new[h]
    y_ref[...] = jnp.concatenate(outs, axis=-1).astype(y_ref.dtype)


def _rwkv(cols, mu, w0, w2, a0, a2, g2, k_k, k_a, r_k, gn_g, gn_b, shift0, s0, ncp, ns):
    t = cols.shape[0]
    nst = ns + 1
    st_in4 = lambda c: (jnp.maximum(c - ncp, 0), 0, 0, 0)
    st_in3 = lambda c: (jnp.maximum(c - ncp, 0), 0, 0)
    st_out4 = lambda c: (jnp.maximum(c - (ncp - 1), 0), 0, 0, 0)
    st_out3 = lambda c: (jnp.maximum(c - (ncp - 1), 0), 0, 0)
    const2 = lambda c: (0, 0)
    row = lambda n: pl.BlockSpec((1, n), const2)
    return pl.pallas_call(
        functools.partial(_rwkv_body, ncp=ncp),
        grid=(ncp + ns,),
        in_specs=[
            pl.BlockSpec((CHUNK, RW_COLS), lambda c: (c, SEG_RW // RW_COLS)),
            row(RW_COLS), row(RW_W),
            pl.BlockSpec((RW_W_RANK, RW_W), const2),
            row(RW_W),
            pl.BlockSpec((RW_A_RANK, RW_W), const2),
            pl.BlockSpec((RW_G_RANK, RW_W), const2),
            row(RW_W), row(RW_W), row(RW_W), row(RW_W), row(RW_W),
            pl.BlockSpec((1, 8, RW_COLS), st_in3),
            pl.BlockSpec((1, RW_HEADS, RW_DH, RW_DH), st_in4),
        ],
        out_specs=[
            pl.BlockSpec((CHUNK, RW_W), lambda c: (c, 0)),
            pl.BlockSpec((1, RW_HEADS, RW_DH, RW_DH), st_out4),
            pl.BlockSpec((1, 8, RW_COLS), st_out3),
        ],
        out_shape=[
            jax.ShapeDtypeStruct((t, RW_W), bf16),
            jax.ShapeDtypeStruct((nst, RW_HEADS, RW_DH, RW_DH), f32),
            jax.ShapeDtypeStruct((nst, 8, RW_COLS), f32),
        ],
        scratch_shapes=[pltpu.VMEM((RW_HEADS, RW_DH, RW_DH), f32), pltpu.VMEM((8, RW_COLS), f32)],
        compiler_params=_params(1),
        name="rwkv",
    )(cols, mu, w0, w2, a0, a2, g2, k_k, k_a, r_k, gn_g, gn_b, shift0, s0)


def _merge_body(x_ref, g_ref, hml_ref, hrw_ref, hgd_ref, bg_ref, wml_ref, wrw_ref, wgd_ref, wout_ref,
                lng_ref, lnb_ref, o_ref):
    d = D_MODEL
    dot = lambda a, b: jnp.dot(a, b, preferred_element_type=f32)
    merged = (jax.nn.sigmoid(g_ref[:, 0:d] + bg_ref[:, 0:d]) * dot(hml_ref[...], wml_ref[...])
              + jax.nn.sigmoid(g_ref[:, d:2 * d] + bg_ref[:, d:2 * d]) * dot(hrw_ref[...], wrw_ref[...])
              + jax.nn.sigmoid(g_ref[:, 2 * d:] + bg_ref[:, 2 * d:]) * dot(hgd_ref[...], wgd_ref[...]))
    y = dot(merged.astype(bf16), wout_ref[...])
    o_ref[...] = _layer_norm(DEEPNORM_ALPHA * x_ref[...] + y, lng_ref[...], lnb_ref[...])


def _merge(x, cols, h_ml, h_rw, h_gd, b_gates, w_ml, w_rw, w_gd, w_out, ln_g, ln_b):
    t, d = x.shape
    tm = _pick(t, (512, 384, 128))
    tok = lambda n: pl.BlockSpec((tm, n), lambda i: (i, 0))
    full = lambda a: pl.BlockSpec(a.shape, lambda i: (0, 0))
    return pl.pallas_call(
        _merge_body,
        grid=(t // tm,),
        in_specs=[tok(d), pl.BlockSpec((tm, GATE_COLS), lambda i: (i, SEG_GATE // GATE_COLS)),
                  tok(ML_W), tok(RW_W), tok(GD_W),
                  full(b_gates), full(w_ml), full(w_rw), full(w_gd), full(w_out), full(ln_g), full(ln_b)],
        out_specs=tok(d),
        out_shape=jax.ShapeDtypeStruct((t, d), f32),
        compiler_params=_params(1),
        name="merge",
    )(x, cols, h_ml, h_rw, h_gd, b_gates, w_ml, w_rw, w_gd, w_out, ln_g, ln_b)


def _route(lt):
    grp = [lt[i:i + 1, :] for i in range(N_GROUPS)]
    gmax = functools.reduce(jnp.maximum, grp)
    gex = [jnp.exp(g - gmax) for g in grp]
    gsum = functools.reduce(lambda a, b: a + b, gex)
    gp = [e / gsum for e in gex]
    p_sel = functools.reduce(jnp.maximum, gp)
    taken = jnp.zeros_like(p_sel)
    sel = []
    for i in range(N_GROUPS):
        s_i = jnp.where(gp[i] == p_sel, 1.0, 0.0) * (1.0 - taken)
        taken = taken + s_i
        sel.append(s_i)
    e_in = []
    for j in range(EXP_PER_GROUP):
        acc = sel[0] * lt[8 + j:9 + j, :]
        for g in range(1, N_GROUPS):
            acc = acc + sel[g] * lt[8 + g * EXP_PER_GROUP + j:9 + g * EXP_PER_GROUP + j, :]
        e_in.append(acc)
    emax = functools.reduce(jnp.maximum, e_in)
    eex = [jnp.exp(e - emax) for e in e_in]
    esum = functools.reduce(lambda a, b: a + b, eex)
    ep = [e / esum for e in eex]
    top = []
    for j in range(EXP_PER_GROUP):
        rank = jnp.zeros_like(p_sel)
        for i in range(EXP_PER_GROUP):
            if i == j:
                continue
            ahead = (ep[i] > ep[j]) if i > j else (ep[i] >= ep[j])
            rank = rank + jnp.where(ahead, 1.0, 0.0)
        top.append(jnp.where(rank < 2.0, ep[j], 0.0))
    tsum = functools.reduce(lambda a, b: a + b, top)
    wts = [tp / tsum * p_sel for tp in top]
    return [sel[g] * wts[j] for g in range(N_GROUPS) for j in range(EXP_PER_GROUP)]


def _ffn_body(x_ref, pe_ref, wr_ref, br_ref, wg_ref, wu_ref, wd_ref, wpg_ref, bpg_ref, wp_ref, lng_ref, lnb_ref,
              o_ref, xb_s, gate_s, acc_s):
    e = pl.program_id(1)
    tm = x_ref.shape[0]

    @pl.when(e == 0)
    def _():
        x = x_ref[...]
        xb_s[...] = x.astype(bf16)
        logits = jnp.dot(x, wr_ref[...], precision=lax.Precision.HIGHEST, preferred_element_type=f32) + br_ref[...]
        rows = _route(logits.T)
        gt = jnp.concatenate(rows + [jnp.zeros((128 - N_EXPERTS, tm), f32)], axis=0)
        gate_s[...] = gt.T

    xb = xb_s[...]
    hg = jnp.dot(xb, wg_ref[0], preferred_element_type=f32)
    hu = jnp.dot(xb, wu_ref[0], preferred_element_type=f32)
    lane = lax.broadcasted_iota(jnp.int32, (tm, 128), 1)
    gcol = jnp.sum(jnp.where(lane == e, gate_s[...], 0.0), axis=-1, keepdims=True)
    hidden = (_silu(hg) * hu * gcol).astype(bf16)
    contrib = jnp.dot(hidden, wd_ref[0], preferred_element_type=f32)

    @pl.when(e == 0)
    def _():
        acc_s[...] = contrib

    @pl.when(e > 0)
    def _():
        acc_s[...] += contrib

    @pl.when(e == N_EXPERTS - 1)
    def _():
        ple = (jax.nn.sigmoid(jnp.dot(xb, wpg_ref[...], preferred_element_type=f32) + bpg_ref[...])
               * jnp.dot(pe_ref[...].astype(bf16), wp_ref[...], preferred_element_type=f32))
        o_ref[...] = _layer_norm(DEEPNORM_ALPHA * x_ref[...] + acc_s[...] + ple, lng_ref[...], lnb_ref[...])


def _ffn(x, pe, w_r, b_r, e_gate, e_up, e_down, w_pg, b_pg, w_p, ln_g, ln_b):
    t, d = x.shape
    tm = _pick(t, (768, 384, 128))
    tok = lambda n: pl.BlockSpec((tm, n), lambda i, e: (i, 0))
    full = lambda a: pl.BlockSpec(a.shape, lambda i, e: (0,) * a.ndim)
    return pl.pallas_call(
        _ffn_body,
        grid=(t // tm, N_EXPERTS),
        in_specs=[tok(d), tok(D_PLE), full(w_r), full(b_r),
                  pl.BlockSpec((1, d, D_EXPERT), lambda i, e: (e, 0, 0)),
                  pl.BlockSpec((1, d, D_EXPERT), lambda i, e: (e, 0, 0)),
                  pl.BlockSpec((1, D_EXPERT, d), lambda i, e: (e, 0, 0)),
                  full(w_pg), full(b_pg), full(w_p), full(ln_g), full(ln_b)],
        out_specs=tok(d),
        out_shape=jax.ShapeDtypeStruct((t, d), f32),
        scratch_shapes=[pltpu.VMEM((tm, d), bf16), pltpu.VMEM((tm, 128), f32), pltpu.VMEM((tm, d), f32)],
        compiler_params=_params(2),
        name="ffn",
    )(x, pe, w_r, b_r, e_gate, e_up, e_down, w_pg, b_pg, w_p, ln_g, ln_b)


def _lane_vec(n, off, vals):
    return jnp.zeros((1, n), f32).at[0, off:off + vals.shape[0]].set(vals)


def _layer(x, pe, st, lw, ncp, ns):
    ml_c, ml_n, ml_m, rw_s, rw_shift, gd_s, gd_conv = st
    w = lw['w_in']
    d = w.shape[0]
    z = lambda n: jnp.zeros((d, n), f32)
    w_proj = jnp.concatenate([
        w[:, 0:ML_MAIN],
        w[:, ML_MAIN:ML_COLS], w[:, ML_COLS + RW_COLS + GD_MAIN:ML_COLS + RW_COLS + GD_COLS], z(SMALL_W - 16),
        w[:, ML_COLS:ML_COLS + RW_COLS], z(SEG_GD - SEG_RW - RW_COLS),
        w[:, ML_COLS + RW_COLS:ML_COLS + RW_COLS + GD_MAIN],
        w[:, ML_COLS + RW_COLS + GD_COLS:],
    ], axis=1).astype(bf16)
    cols = _inproj(x, w_proj)

    gate_bias = jnp.concatenate([lw['ml_i_bias'], lw['ml_f_bias']])
    cn0 = jnp.concatenate([ml_c, ml_n[..., None], jnp.zeros(ml_c.shape[:-1] + (ML_DV - 1,), f32)], axis=-1)
    h_ml, cn1, m1 = _mlstm(cols, _lane_vec(128, 0, gate_bias), _lane_vec(16, 0, gate_bias).T,
                           lw['ml_norm'][None, :], cn0, ml_m[..., None, None], ncp, ns)
    row = lambda a: a.reshape(1, -1)
    shift0 = jnp.concatenate([jnp.zeros((ns, 7, RW_COLS), f32), rw_shift], axis=1)
    h_rw, rws1, shift1 = _rwkv(cols, row(lw['rw_mu']), row(lw['rw_w0']), lw['rw_w2'].astype(bf16), row(lw['rw_a0']),
                               lw['rw_a2'].astype(bf16), lw['rw_g2'].astype(bf16), row(lw['rw_k_k']), row(lw['rw_k_a']),
                               row(lw['rw_r_k']), row(lw['rw_gn_g']), row(lw['rw_gn_b']), shift0, rw_s, ncp, ns)
    conv0 = jnp.concatenate([jnp.zeros((ns, 8 - (GD_CONV - 1), GD_CONV_CH), f32), gd_conv], axis=1)
    convw = jnp.concatenate([lw['gd_conv_w'], jnp.zeros((8 - GD_CONV, GD_CONV_CH), f32)], axis=0)
    h_gd, gds1, conv1 = _gdn(cols, convw, _lane_vec(128, 12, lw['gd_a_log']), _lane_vec(16, 12, lw['gd_a_log']).T,
                             _lane_vec(128, 12, lw['gd_dt_bias']), _lane_vec(16, 12, lw['gd_dt_bias']).T,
                             lw['gd_norm'][None, :], conv0, gd_s, ncp, ns)

    x = _merge(x, cols, h_ml, h_rw, h_gd, lw['b_gates'].reshape(1, -1), lw['w_br_ml'].astype(bf16),
               lw['w_br_rw'].astype(bf16), lw['w_br_gd'].astype(bf16), lw['w_out'].astype(bf16),
               row(lw['ln1_g']), row(lw['ln1_b']))

    w_r = jnp.concatenate([lw['w_rg'], z(8 - N_GROUPS), lw['w_re'], z(128 - 8 - N_EXPERTS)], axis=1)
    b_r = jnp.concatenate([lw['b_rg'], jnp.zeros((8 - N_GROUPS,), f32), lw['b_re'],
                           jnp.zeros((128 - 8 - N_EXPERTS,), f32)])[None, :]
    x = _ffn(x, pe, w_r, b_r, lw['e_gate'].astype(bf16), lw['e_up'].astype(bf16), lw['e_down'].astype(bf16),
             lw['ple_gate_w'].astype(bf16), row(lw['ple_gate_b']), lw['ple_w'].astype(bf16),
             row(lw['ln2_g']), row(lw['ln2_b']))

    new_st = (cn1[..., :ML_DV], cn1[..., ML_DV], m1[..., 0, 0], rws1, shift1[:, 7:8, :], gds1,
              conv1[:, 8 - (GD_CONV - 1):, :])
    return x, new_st


_WEIGHT_NAMES = ('w_in', 'ml_i_bias', 'ml_f_bias', 'ml_norm', 'rw_mu', 'rw_w0', 'rw_w2', 'rw_a0', 'rw_a2', 'rw_g2',
                 'rw_k_k', 'rw_k_a', 'rw_r_k', 'rw_gn_g', 'rw_gn_b', 'gd_conv_w', 'gd_a_log', 'gd_dt_bias', 'gd_norm',
                 'w_br_ml', 'w_br_rw', 'w_br_gd', 'b_gates', 'w_out', 'ln1_g', 'ln1_b', 'w_rg', 'b_rg', 'w_re', 'b_re',
                 'e_gate', 'e_up', 'e_down', 'ple_w', 'ple_gate_w', 'ple_gate_b', 'ln2_g', 'ln2_b')


def _trunk(x_prompt, x_sample, p_prompt, p_sample, states, weights):
    bp, seq, d = x_prompt.shape
    bs, dseq, _ = x_sample.shape
    assert bp == 1 and seq % CHUNK == 0 and dseq == CHUNK
    ncp, ns = seq // CHUNK, bs
    depth = p_prompt.shape[0]
    x = jnp.concatenate([x_prompt.reshape(seq, d), x_sample.reshape(bs * dseq, d)], axis=0)
    new_states = []
    for l in range(depth):
        pe = jnp.concatenate([p_prompt[l].reshape(seq, -1), p_sample[l].reshape(bs * dseq, -1)], axis=0)
        lw = {k: v[l] for k, v in weights.items()}
        x, st = _layer(x, pe, tuple(s[l] for s in states), lw, ncp, ns)
        new_states.append(st)
    y_p = x[:seq].reshape(bp, seq, d)
    y_s = x[seq:].reshape(bs, dseq, d)
    stacked = [jnp.stack([st[j] for st in new_states]) for j in range(7)]
    outs_p = [s[:, :1] for s in stacked]
    outs_s = [s[:, 1:] for s in stacked]
    return (y_p, y_s, *outs_p, *outs_s)


def kernel(x_prompt, x_sample, p_prompt, p_sample, state_mlstm_c, state_mlstm_n, state_mlstm_m, state_rwkv_s, state_rwkv_shift, state_gdn_s, state_gdn_conv, w_in, ml_i_bias, ml_f_bias, ml_norm, rw_mu, rw_w0, rw_w2, rw_a0, rw_a2, rw_g2, rw_k_k, rw_k_a, rw_r_k, rw_gn_g, rw_gn_b, gd_conv_w, gd_a_log, gd_dt_bias, gd_norm, w_br_ml, w_br_rw, w_br_gd, b_gates, w_out, ln1_g, ln1_b, w_rg, b_rg, w_re, b_re, e_gate, e_up, e_down, ple_w, ple_gate_w, ple_gate_b, ln2_g, ln2_b):
    states = (state_mlstm_c, state_mlstm_n, state_mlstm_m, state_rwkv_s, state_rwkv_shift, state_gdn_s, state_gdn_conv)
    wvals = (w_in, ml_i_bias, ml_f_bias, ml_norm, rw_mu, rw_w0, rw_w2, rw_a0, rw_a2, rw_g2, rw_k_k, rw_k_a, rw_r_k,
             rw_gn_g, rw_gn_b, gd_conv_w, gd_a_log, gd_dt_bias, gd_norm, w_br_ml, w_br_rw, w_br_gd, b_gates, w_out,
             ln1_g, ln1_b, w_rg, b_rg, w_re, b_re, e_gate, e_up, e_down, ple_w, ple_gate_w, ple_gate_b, ln2_g, ln2_b)
    return _trunk(x_prompt, x_sample, p_prompt, p_sample, states, dict(zip(_WEIGHT_NAMES, wvals)))
```

```python
import functools

import jax
import jax.numpy as jnp
from jax import lax
from jax.experimental import pallas as pl
from jax.experimental.pallas import tpu as pltpu

f32 = jnp.float32
bf16 = jnp.bfloat16

D_MODEL = 1024
DEPTH = 4
CHUNK = 64
CHUNK_SHIFT = 6
D_PLE = 256
ML_HEADS, ML_DQK, ML_DV = 4, 64, 128
ML_W = ML_HEADS * ML_DV
RW_HEADS, RW_DH = 8, 64
RW_W = RW_HEADS * RW_DH
RW_W_RANK, RW_A_RANK, RW_G_RANK = 64, 64, 128
GD_HEADS, GD_DK, GD_DV = 4, 128, 128
GD_W = GD_HEADS * GD_DV
GD_CONV = 4
N_GROUPS, EXP_PER_GROUP = 4, 4
N_EXPERTS = N_GROUPS * EXP_PER_GROUP
D_EXPERT = 256
ML_COLS = 2 * ML_HEADS * ML_DQK + 2 * ML_W + 2 * ML_HEADS
RW_COLS = 3 * RW_W + RW_W_RANK + RW_A_RANK + RW_G_RANK
GD_CONV_CH = 2 * GD_HEADS * GD_DK + GD_W
GD_COLS = GD_CONV_CH + GD_W + 2 * GD_HEADS
GATE_COLS = 3 * D_MODEL
DEEPNORM_ALPHA = (2 * DEPTH) ** 0.25
LN_EPS = 1e-5
GN_EPS = 64e-5
D_EXPERT_SHIFT = 8
assert CHUNK == 1 << CHUNK_SHIFT and D_EXPERT == 1 << D_EXPERT_SHIFT

ML_MAIN = 2 * ML_HEADS * ML_DQK + 2 * ML_W
SMALL_W = 256
GD_MAIN = GD_CONV_CH + GD_W
SEG_ML = 0
SEG_SMALL = ML_MAIN
SEG_RW = SEG_SMALL + SMALL_W
SEG_GD = 2 * GD_MAIN
SEG_GATE = SEG_GD + GD_MAIN
PROJ_COLS = SEG_GATE + GATE_COLS
assert SEG_RW == RW_COLS and SEG_GATE == 2 * GATE_COLS and SEG_GD >= SEG_RW + RW_COLS

VMEM_LIMIT = 56 * 1024 * 1024
MAX_CHUNKS_PER_STEP = 4

_NN = (((1,), (0,)), ((), ()))
_NT = (((1,), (1,)), ((), ()))
_TN = (((0,), (0,)), ((), ()))


def _dot(a, b, dims=_NN):
    return lax.dot_general(a, b, dims, preferred_element_type=f32)


def _mm(a, b, dims=_NN):
    return _dot(a.astype(bf16), b.astype(bf16), dims)


def _split3(x):
    h1 = x.astype(bf16)
    r1 = x - h1.astype(f32)
    h2 = r1.astype(bf16)
    h3 = (r1 - h2.astype(f32)).astype(bf16)
    return h1, h2, h3


def _cumsum_rows(tri, x):
    h1, h2, h3 = _split3(x)
    return _dot(tri, h1) + (_dot(tri, h2) + _dot(tri, h3))


def _cumsum_lanes(x, tri_t):
    h1, h2, h3 = _split3(x)
    return _dot(h1, tri_t) + (_dot(h2, tri_t) + _dot(h3, tri_t))


def _softplus(x):
    return jnp.maximum(x, 0.0) + jnp.log1p(jnp.exp(-jnp.abs(x)))


def _log_sigmoid(x):
    return -_softplus(-x)


def _silu(x):
    return x * jax.nn.sigmoid(x)


def _iotas(n):
    r = lax.broadcasted_iota(jnp.int32, (n, n), 0)
    c = lax.broadcasted_iota(jnp.int32, (n, n), 1)
    return r, c


def _chunk_tris(nr):
    r, c = _iotas(nr)
    same = (r >> CHUNK_SHIFT) == (c >> CHUNK_SHIFT)
    return (same & (r >= c)).astype(bf16), (same & (r <= c)).astype(bf16)


def _tri_inv_many(mats, r, c):
    eye = (r == c).astype(f32)
    blk8 = (r >> 3) == (c >> 3)
    ns = [jnp.where(blk8, a, 0.0) for a in mats]
    nb = [n.astype(bf16) for n in ns]
    n2 = [_dot(p, p) for p in nb]
    n2b = [x.astype(bf16) for x in n2]
    n3 = [_dot(p, q) for p, q in zip(nb, n2b)]
    n4 = [_dot(q, q) for q in n2b]
    ps = [eye - n + m2 - m3 for n, m2, m3 in zip(ns, n2, n3)]
    xs = [p + _dot(p.astype(bf16), m4.astype(bf16)) for p, m4 in zip(ps, n4)]
    for sh in (3, 4, 5):
        off = ((r >> (sh + 1)) == (c >> (sh + 1))) & ((r >> sh) != (c >> sh))
        xb = [x.astype(bf16) for x in xs]
        ts = [_dot(jnp.where(off, a, 0.0).astype(bf16), q) for a, q in zip(mats, xb)]
        xs = [x - _dot(q, t.astype(bf16)) for x, q, t in zip(xs, xb, ts)]
    return xs


def _layer_norm(x, g, b):
    mu = jnp.mean(x, axis=-1, keepdims=True)
    xc = x - mu
    var = jnp.mean(xc * xc, axis=-1, keepdims=True)
    return xc * lax.rsqrt(var + LN_EPS) * g + b


def _params(n_axes):
    return pltpu.CompilerParams(dimension_semantics=("arbitrary",) * n_axes, vmem_limit_bytes=VMEM_LIMIT)


def _pick(n, cands):
    for t in cands:
        if n % t == 0:
            return t
    raise ValueError(f"no tile for {n} in {cands}")


def _chunks_per_step(n):
    return _pick(n, tuple(k for k in (4, 2, 1) if k <= MAX_CHUNKS_PER_STEP))


def _inproj_body(x_ref, w_ref, o_ref, xb_s):
    @pl.when(pl.program_id(1) == 0)
    def _():
        xb_s[...] = x_ref[...].astype(bf16)

    o_ref[...] = jnp.dot(xb_s[...], w_ref[...], preferred_element_type=f32)


def _inproj(x, w):
    t, k = x.shape
    n = w.shape[1]
    tm = _pick(t, (1536, 768, 512, 384, 128))
    tn = 512
    return pl.pallas_call(
        _inproj_body,
        grid=(t // tm, n // tn),
        in_specs=[pl.BlockSpec((tm, k), lambda i, j: (i, 0)), pl.BlockSpec((k, tn), lambda i, j: (0, j))],
        out_specs=pl.BlockSpec((tm, tn), lambda i, j: (i, j)),
        out_shape=jax.ShapeDtypeStruct((t, n), f32),
        scratch_shapes=[pltpu.VMEM((tm, k), bf16)],
        compiler_params=_params(2),
        name="inproj",
    )(x, w)


def _mixer_call(body, name, cols, col_specs, params, states, state_shapes, out_width, n_chunks, chunk_off):
    carry = states is None
    nch = _chunks_per_step(n_chunks)
    nr = nch * CHUNK
    assert (chunk_off * CHUNK) % nr == 0
    off = chunk_off * CHUNK // nr
    nz = lambda shp: (0,) * len(shp)
    in_specs = [pl.BlockSpec((nr, w), lambda i, cb=cb: (off + i, cb)) for w, cb in col_specs]
    in_specs += [pl.BlockSpec(p.shape, lambda i, k=p.ndim: (0,) * k) for p in params]
    args = [cols] * len(col_specs) + list(params)
    if carry:
        st_specs = [pl.BlockSpec((1,) + shp, lambda i, shp=shp: (0,) + nz(shp)) for shp in state_shapes]
        n_st = 1
        scratch = [pltpu.VMEM(shp, f32) for shp in state_shapes]
    else:
        st_specs = [pl.BlockSpec((nch,) + shp, lambda i, shp=shp: (i,) + nz(shp)) for shp in state_shapes]
        in_specs += st_specs
        args += list(states)
        n_st = n_chunks
        scratch = []
    return pl.pallas_call(
        functools.partial(body, nch=nch, carry=carry),
        grid=(n_chunks // nch,),
        in_specs=in_specs,
        out_specs=[pl.BlockSpec((nr, out_width), lambda i: (i, 0))] + st_specs,
        out_shape=[jax.ShapeDtypeStruct((n_chunks * CHUNK, out_width), bf16)]
        + [jax.ShapeDtypeStruct((n_st,) + shp, f32) for shp in state_shapes],
        scratch_shapes=scratch,
        compiler_params=_params(1),
        name=name + ("_seq" if carry else "_chunks"),
    )(*args)


def _run_mixer(body, name, cols, col_specs, params, states, state_shapes, out_width, ncp, ns):
    outs_p = _mixer_call(body, name, cols, col_specs, params, None, state_shapes, out_width, ncp, 0)
    outs_s = _mixer_call(body, name, cols, col_specs, params, states, state_shapes, out_width, ns, ncp)
    return [jnp.concatenate([a, b], axis=0) for a, b in zip(outs_p, outs_s)]


def _init_carried(carry, scratch_refs):
    if carry:
        @pl.when(pl.program_id(0) == 0)
        def _():
            for s in scratch_refs:
                s[...] = jnp.zeros_like(s)


def _shifted_rows(x, head_rows, nch, carry, shift):
    L = CHUNK
    if carry:
        full = jnp.concatenate([head_rows(0), x], axis=0)
        return full[8 - shift:8 - shift + x.shape[0]]
    parts = [jnp.concatenate([head_rows(j), x[j * L:(j + 1) * L]], axis=0)[8 - shift:8 - shift + L] for j in range(nch)]
    return jnp.concatenate(parts, axis=0) if nch > 1 else parts[0]


def _mlstm_body(*refs, nch, carry):
    if carry:
        main_ref, sm_ref, brow_ref, bcol_ref, norm_ref, h_ref, cn_out_ref, m_out_ref, cn_s, m_s = refs
    else:
        main_ref, sm_ref, brow_ref, bcol_ref, norm_ref, cn0_ref, m0_ref, h_ref, cn_out_ref, m_out_ref = refs
        cn_s = m_s = None
    L = CHUNK
    nr = nch * L
    _init_carried(carry, (cn_s, m_s))
    hs, js = range(ML_HEADS), range(nch)
    rows = [slice(j * L, (j + 1) * L) for j in js]

    r, cc = _iotas(L)
    causal = r >= cc
    tri, tri_t = _chunk_tris(nr)
    one_lane0 = (lax.broadcasted_iota(jnp.int32, (L, ML_DV), 1) == 0).astype(f32)

    sm = sm_ref[:, :128]
    g_c = sm + brow_ref[...]
    g_r = sm.T[:16, :] + bcol_ref[...]
    b_c = _cumsum_rows(tri, _log_sigmoid(g_c))
    b_r = _cumsum_lanes(_log_sigmoid(g_r), tri_t)
    cn_in = [cn_s[h] for h in hs] if carry else None
    m_in = [m_s[h] for h in hs] if carry else None

    bc = [[b_c[rows[j], 4 + h:5 + h] for h in hs] for j in js]
    d_intra = [[jnp.where(causal, bc[j][h] - b_r[4 + h:5 + h, rows[j]] + g_r[h:h + 1, rows[j]], -jnp.inf)
                for h in hs] for j in js]
    d_max = [[jnp.max(d_intra[j][h], axis=-1, keepdims=True) for h in hs] for j in js]
    b_last = [[bc[j][h][L - 1:L, :] for h in hs] for j in js]
    d_last = [[b_last[j][h] - bc[j][h] + g_c[rows[j], h:h + 1] for h in hs] for j in js]
    dl_max = [[jnp.max(d_last[j][h], axis=0, keepdims=True) for h in hs] for j in js]
    m_prev, m_new = [], []
    for j in js:
        m_prev.append((m_in if j == 0 else m_new[j - 1]) if carry else [m0_ref[j, h] for h in hs])
        m_new.append([jnp.maximum(b_last[j][h] + m_prev[j][h], dl_max[j][h]) for h in hs])
    m_t = [[jnp.maximum(bc[j][h] + m_prev[j][h], d_max[j][h]) for h in hs] for j in js]

    qb = [[(main_ref[rows[j], h * ML_DQK:(h + 1) * ML_DQK] * (ML_DQK ** -0.5)).astype(bf16) for h in hs] for j in js]
    kb = [[main_ref[rows[j], 256 + h * ML_DQK:256 + (h + 1) * ML_DQK].astype(bf16) for h in hs] for j in js]
    v = [[main_ref[rows[j], 512 + h * ML_DV:512 + (h + 1) * ML_DV] for h in hs] for j in js]
    qk = [[_dot(qb[j][h], kb[j][h], _NT) for h in hs] for j in js]
    kv = [[_dot(kb[j][h], (jnp.concatenate([v[j][h], one_lane0], axis=-1)
                           * jnp.exp(d_last[j][h] - m_new[j][h])).astype(bf16), _TN) for h in hs] for j in js]
    s = [[qk[j][h] * jnp.exp(d_intra[j][h] - m_t[j][h]) for h in hs] for j in js]
    sv = [[_mm(s[j][h], v[j][h]) for h in hs] for j in js]
    cn, cn_new = [], []
    for j in js:
        cn.append((cn_in if j == 0 else cn_new[j - 1]) if carry else [cn0_ref[j, h] for h in hs])
        cn_new.append([jnp.exp(b_last[j][h] + m_prev[j][h] - m_new[j][h]) * cn[j][h] + kv[j][h] for h in hs])
    qcn = [[_dot(qb[j][h], cn[j][h].astype(bf16)) for h in hs] for j in js]

    o_gate = jax.nn.sigmoid(main_ref[:, 1024:1024 + ML_W]) * norm_ref[...]
    jh = [(j, h) for j in js for h in hs]
    s_sum = [jnp.sum(s[j][h], axis=-1, keepdims=True) for j, h in jh]
    w_inter = [jnp.exp(bc[j][h] + m_prev[j][h] - m_t[j][h]) for j, h in jh]
    r_den = [1.0 / jnp.maximum(jnp.abs(w_inter[i] * qcn[j][h][:, ML_DV:ML_DV + 1] + s_sum[i]), jnp.exp(-m_t[j][h]))
             for i, (j, h) in enumerate(jh)]
    hh = [(w_inter[i] * qcn[j][h][:, :ML_DV] + sv[j][h]) * r_den[i] for i, (j, h) in enumerate(jh)]
    ms = [jnp.mean(x_ * x_, axis=-1, keepdims=True) for x_ in hh]
    outs = [(hh[i] * lax.rsqrt(ms[i] + 1e-6) * o_gate[rows[j], h * ML_DV:(h + 1) * ML_DV]).astype(h_ref.dtype)
            for i, (j, h) in enumerate(jh)]

    for j in js:
        for h in hs:
            h_ref[rows[j], h * ML_DV:(h + 1) * ML_DV] = outs[j * ML_HEADS + h]
    for h in hs:
        if carry:
            cn_s[h] = cn_new[nch - 1][h]
            m_s[h] = m_new[nch - 1][h]
            cn_out_ref[0, h] = cn_new[nch - 1][h]
            m_out_ref[0, h] = m_new[nch - 1][h]
        else:
            for j in js:
                cn_out_ref[j, h] = cn_new[j][h]
                m_out_ref[j, h] = m_new[j][h]


def _mlstm(cols, brow, bcol, norm, cn0, m0, ncp, ns):
    return _run_mixer(_mlstm_body, "mlstm", cols, [(ML_MAIN, SEG_ML // ML_MAIN), (SMALL_W, SEG_SMALL // SMALL_W)],
                      [brow, bcol, norm], [cn0, m0], [(ML_HEADS, ML_DQK, 2 * ML_DV), (ML_HEADS, 1, 1)], ML_W, ncp, ns)


def _gdn_body(*refs, nch, carry):
    if carry:
        (main_ref, sm_ref, convw_ref, arow_ref, acol_ref, dtrow_ref, dtcol_ref, norm_ref,
         o_ref, s_out_ref, conv_out_ref, s_s, prev_s) = refs
    else:
        (main_ref, sm_ref, convw_ref, arow_ref, acol_ref, dtrow_ref, dtcol_ref, norm_ref, s0_ref, conv0_ref,
         o_ref, s_out_ref, conv_out_ref) = refs
        s_s = prev_s = None
    L = CHUNK
    nr = nch * L
    _init_carried(carry, (s_s, prev_s))
    hs, js = range(GD_HEADS), range(nch)
    rows = [slice(j * L, (j + 1) * L) for j in js]

    r, cc = _iotas(L)
    incl = r >= cc
    strict = r > cc
    tri, tri_t = _chunk_tris(nr)

    x = main_ref[:, :GD_CONV_CH]
    s_in = [s_s[h] for h in hs] if carry else None
    head_rows = (lambda j: prev_s[...]) if carry else (lambda j: conv0_ref[j])
    conv = convw_ref[3:4, :] * x
    for d in (1, 2, 3):
        conv = conv + convw_ref[3 - d:4 - d, :] * _shifted_rows(x, head_rows, nch, carry, d)
    conv = _silu(conv)

    sm = sm_ref[:, :128]
    beta_all = jax.nn.sigmoid(sm)
    la_c = -jnp.exp(arow_ref[...]) * _softplus(sm + dtrow_ref[...])
    la_r = -jnp.exp(acol_ref[...]) * _softplus(sm.T[:16, :] + dtcol_ref[...])
    gc_c = _cumsum_rows(tri, la_c)
    gc_r = _cumsum_lanes(la_r, tri_t)

    q, k, v, beta, gcc, dec = ([[None] * GD_HEADS for _ in js] for _ in range(6))
    q_raw = [[conv[rows[j], h * GD_DK:(h + 1) * GD_DK] for h in hs] for j in js]
    k_raw = [[conv[rows[j], 512 + h * GD_DK:512 + (h + 1) * GD_DK] for h in hs] for j in js]
    q_ss = [[jnp.sum(q_raw[j][h] * q_raw[j][h], axis=-1, keepdims=True) for h in hs] for j in js]
    k_ss = [[jnp.sum(k_raw[j][h] * k_raw[j][h], axis=-1, keepdims=True) for h in hs] for j in js]
    for j in js:
        for h in hs:
            q[j][h] = q_raw[j][h] * (lax.rsqrt(q_ss[j][h] + 1e-6) * (GD_DK ** -0.5))
            k[j][h] = k_raw[j][h] * lax.rsqrt(k_ss[j][h] + 1e-6)
            v[j][h] = conv[rows[j], 1024 + h * GD_DV:1024 + (h + 1) * GD_DV]
            beta[j][h] = beta_all[rows[j], 8 + h:9 + h]
            gcc[j][h] = gc_c[rows[j], 12 + h:13 + h]
            dec[j][h] = jnp.exp(jnp.where(incl, gcc[j][h] - gc_r[12 + h:13 + h, rows[j]], -jnp.inf))
    jh = [(j, h) for j in js for h in hs]
    kb = {p: k[p[0]][p[1]].astype(bf16) for p in jh}
    kk = {p: _dot(kb[p], kb[p], _NT) for p in jh}
    qk = {p: (_dot(q[p[0]][p[1]].astype(bf16), kb[p], _NT) * dec[p[0]][p[1]]).astype(bf16) for p in jh}
    a_mat = [jnp.where(strict, beta[j][h] * kk[(j, h)] * dec[j][h], 0.0) for j, h in jh]
    rhs = {p: jnp.concatenate([v[p[0]][p[1]] * beta[p[0]][p[1]],
                               k[p[0]][p[1]] * (beta[p[0]][p[1]] * jnp.exp(gcc[p[0]][p[1]]))], axis=-1) for p in jh}
    tinv = dict(zip(jh, _tri_inv_many(a_mat, r, cc)))
    sol = {p: _mm(tinv[p], rhs[p]) for p in jh}
    qg = {p: (q[p[0]][p[1]] * jnp.exp(gcc[p[0]][p[1]])).astype(bf16) for p in jh}
    g_last = {p: gcc[p[0]][p[1]][L - 1:L, :] for p in jh}
    kd = {p: (k[p[0]][p[1]] * jnp.exp(g_last[p] - gcc[p[0]][p[1]])).astype(bf16) for p in jh}

    z_gate = _silu(main_ref[:, GD_CONV_CH:])
    s_new = []
    o_all = []
    for j in js:
        s_cur = (s_in if j == 0 else s_new[j - 1]) if carry else [s0_ref[j, h] for h in hs]
        sb = [x_.astype(bf16) for x_ in s_cur]
        ub = [(sol[(j, h)][:, :GD_DV] - _dot(sol[(j, h)][:, GD_DV:].astype(bf16), sb[h])).astype(bf16) for h in hs]
        s_new.append([jnp.exp(g_last[(j, h)]) * s_cur[h] + _dot(kd[(j, h)], ub[h], _TN) for h in hs])
        o_all.extend(_dot(qg[(j, h)], sb[h]) + _dot(qk[(j, h)], ub[h]) for h in hs)
    o_ms = [jnp.mean(x_ * x_, axis=-1, keepdims=True) for x_ in o_all]
    outs = [(o_all[i] * lax.rsqrt(o_ms[i] + 1e-6) * norm_ref[...] * z_gate[rows[j], h * GD_DV:(h + 1) * GD_DV])
            .astype(o_ref.dtype) for i, (j, h) in enumerate(jh)]

    for j in js:
        for h in hs:
            o_ref[rows[j], h * GD_DV:(h + 1) * GD_DV] = outs[j * GD_HEADS + h]
    if carry:
        tail = x[nr - 8:, :]
        prev_s[...] = tail
        conv_out_ref[0] = tail
        for h in hs:
            s_s[h] = s_new[nch - 1][h]
            s_out_ref[0, h] = s_new[nch - 1][h]
    else:
        for j in js:
            conv_out_ref[j] = x[(j + 1) * L - 8:(j + 1) * L, :]
            for h in hs:
                s_out_ref[j, h] = s_new[j][h]


def _gdn(cols, convw, arow, acol, dtrow, dtcol, norm, conv0, s0, ncp, ns):
    return _run_mixer(_gdn_body, "gdn", cols, [(GD_MAIN, SEG_GD // GD_MAIN), (SMALL_W, SEG_SMALL // SMALL_W)],
                      [convw, arow, acol, dtrow, dtcol, norm], [s0, conv0],
                      [(GD_HEADS, GD_DK, GD_DV), (8, GD_CONV_CH)], GD_W, ncp, ns)


def _rwkv_body(*refs, nch, carry):
    if carry:
        (rw_ref, mu_ref, w0_ref, w2_ref, a0_ref, a2_ref, g2_ref, kk_ref, ka_ref, rk_ref, gng_ref, gnb_ref,
         y_ref, s_out_ref, shift_out_ref, s_s, prev_s) = refs
    else:
        (rw_ref, mu_ref, w0_ref, w2_ref, a0_ref, a2_ref, g2_ref, kk_ref, ka_ref, rk_ref, gng_ref, gnb_ref,
         s0_ref, shift0_ref, y_ref, s_out_ref, shift_out_ref) = refs
        s_s = prev_s = None
    L = CHUNK
    nr = nch * L
    dh = RW_DH
    _init_carried(carry, (s_s, prev_s))
    hs, js = range(RW_HEADS), range(nch)
    rows = [slice(j * L, (j + 1) * L) for j in js]
    sls = [slice(h * dh, (h + 1) * dh) for h in hs]

    r_i, c_i = _iotas(L)
    incl = r_i >= c_i
    strict = r_i > c_i
    tri, _ = _chunk_tris(nr)

    x = rw_ref[...]
    s_in = [s_s[h] for h in hs] if carry else None
    head_rows = (lambda j: prev_s[...]) if carry else (lambda j: shift0_ref[j])
    prev = _shifted_rows(x, head_rows, nch, carry, 1)
    mixed = x + (prev - x) * mu_ref[...]
    r_all = mixed[:, 0:RW_W]
    k_all = mixed[:, RW_W:2 * RW_W]
    v_all = mixed[:, 2 * RW_W:3 * RW_W]
    w_lo = mixed[:, 3 * RW_W:3 * RW_W + RW_W_RANK]
    a_lo = mixed[:, 3 * RW_W + RW_W_RANK:3 * RW_W + RW_W_RANK + RW_A_RANK]
    g_lo = mixed[:, 3 * RW_W + RW_W_RANK + RW_A_RANK:]
    w_log = -_softplus(-(w0_ref[...] + _mm(jnp.tanh(w_lo), w2_ref[...]))) - 0.5
    lw = -jnp.exp(w_log)
    a_all = jax.nn.sigmoid(a0_ref[...] + _mm(a_lo, a2_ref[...]))
    g_all = _mm(jax.nn.sigmoid(g_lo), g2_ref[...])
    cum = _cumsum_rows(tri, lw)
    cum_ex = cum - lw

    seg_r = lax.broadcasted_iota(jnp.int32, (RW_W, 128), 0)
    seg_c = lax.broadcasted_iota(jnp.int32, (RW_W, 128), 1)
    seg = ((seg_r >> 6) == seg_c).astype(bf16)
    seg_tr = lax.broadcasted_iota(jnp.int32, (128, RW_W), 0)
    seg_tc = lax.broadcasted_iota(jnp.int32, (128, RW_W), 1)
    seg_t = (seg_tr == (seg_tc >> 6)).astype(bf16)
    head_sum = lambda a: _dot(a.astype(bf16), seg)
    head_bcast = lambda a: _dot(a.astype(bf16), seg_t)

    kk_all = k_all * kk_ref[...]
    kk_all = kk_all * head_bcast(lax.rsqrt(head_sum(kk_all * kk_all) + 1e-6))
    kmod_all = k_all * (1.0 + (a_all - 1.0) * ka_ref[...])
    e_neg = jnp.exp(-cum)
    rt_all = r_all * jnp.exp(cum)
    kt_all = kmod_all * e_neg
    at_all = -kk_all * jnp.exp(cum_ex)
    bt_all = kk_all * a_all * e_neg
    bonus_all = head_bcast(head_sum(r_all * kmod_all * rk_ref[...])) * v_all

    jh = [(j, h) for j in js for h in hs]
    piece = lambda a, p: a[rows[p[0]], sls[p[1]]]
    atb = {p: piece(at_all, p).astype(bf16) for p in jh}
    rtb = {p: piece(rt_all, p).astype(bf16) for p in jh}
    ktb = {p: piece(kt_all, p).astype(bf16) for p in jh}
    btb = {p: piece(bt_all, p).astype(bf16) for p in jh}
    vtb = {p: piece(v_all, p).T.astype(bf16) for p in jh}
    gmat = {p: _dot(jnp.concatenate([atb[p], rtb[p]], axis=0), jnp.concatenate([ktb[p], btb[p]], axis=0), _NT)
            for p in jh}
    mk = {p: jnp.concatenate([jnp.where(strict, gmat[p][:L, :L], 0.0), jnp.where(incl, gmat[p][L:, :L], 0.0)],
                             axis=0).astype(bf16) for p in jh}
    n_ab = [jnp.where(strict, -gmat[p][:L, L:], 0.0) for p in jh]
    n_rb = {p: jnp.where(incl, gmat[p][L:, L:], 0.0).astype(bf16) for p in jh}
    mkv_t = {p: _dot(vtb[p], mk[p], _NT) for p in jh}
    kv_s = {p: _dot(vtb[p], ktb[p]) for p in jh}
    tinv = {p: t.astype(bf16) for p, t in zip(jh, _tri_inv_many(n_ab, r_i, c_i))}
    tar = {p: jnp.concatenate([_dot(tinv[p], atb[p]).astype(bf16), rtb[p]], axis=0) for p in jh}
    m_t = {p: _dot(mkv_t[p][:, :L].astype(bf16), tinv[p], _NT) for p in jh}

    s_new = []
    y_rows = []
    for j in js:
        s_cur = (s_in if j == 0 else s_new[j - 1]) if carry else [s0_ref[j, h] for h in hs]
        st = [_dot(s_cur[h].astype(bf16), tar[(j, h)], _NT) for h in hs]
        ut = [(st[h][:, :L] + m_t[(j, h)]).astype(bf16) for h in hs]
        p_last = jnp.exp(cum[(j + 1) * L - 1:(j + 1) * L, :])
        s_new.append([(s_cur[h] + _dot(ut[h], btb[(j, h)]) + kv_s[(j, h)]) * p_last[:, sls[h]] for h in hs])
        yt = [st[h][:, L:] + mkv_t[(j, h)][:, L:] + _dot(ut[h], n_rb[(j, h)], _NT) for h in hs]
        y_rows.append(jnp.concatenate([yt[h].T for h in hs], axis=-1))

    y_all = jnp.concatenate(y_rows, axis=0) if nch > 1 else y_rows[0]
    mu_y = head_bcast(head_sum(y_all) * (1.0 / dh))
    yc = y_all - mu_y
    rstd = head_bcast(lax.rsqrt(head_sum(yc * yc) * (1.0 / dh) + GN_EPS))
    yn = yc * rstd * gng_ref[...] + gnb_ref[...] + bonus_all
    y_ref[...] = (yn * g_all).astype(y_ref.dtype)

    if carry:
        tail = x[nr - 8:, :]
        prev_s[...] = tail
        shift_out_ref[0] = tail
        for h in hs:
            s_s[h] = s_new[nch - 1][h]
            s_out_ref[0, h] = s_new[nch - 1][h]
    else:
        for j in js:
            shift_out_ref[j] = x[(j + 1) * L - 8:(j + 1) * L, :]
            for h in hs:
                s_out_ref[j, h] = s_new[j][h]


def _rwkv(cols, mu, w0, w2, a0, a2, g2, k_k, k_a, r_k, gn_g, gn_b, shift0, s0, ncp, ns):
    return _run_mixer(_rwkv_body, "rwkv", cols, [(RW_COLS, SEG_RW // RW_COLS)],
                      [mu, w0, w2, a0, a2, g2, k_k, k_a, r_k, gn_g, gn_b], [s0, shift0],
                      [(RW_HEADS, RW_DH, RW_DH), (8, RW_COLS)], RW_W, ncp, ns)


def _merge_body(x_ref, g_ref, hml_ref, hrw_ref, hgd_ref, bg_ref, wml_ref, wrw_ref, wgd_ref, wout_ref,
                lng_ref, lnb_ref, o_ref):
    d = D_MODEL
    dot = lambda a, b: jnp.dot(a, b, preferred_element_type=f32)
    merged = (jax.nn.sigmoid(g_ref[:, 0:d] + bg_ref[:, 0:d]) * dot(hml_ref[...], wml_ref[...])
              + jax.nn.sigmoid(g_ref[:, d:2 * d] + bg_ref[:, d:2 * d]) * dot(hrw_ref[...], wrw_ref[...])
              + jax.nn.sigmoid(g_ref[:, 2 * d:] + bg_ref[:, 2 * d:]) * dot(hgd_ref[...], wgd_ref[...]))
    y = dot(merged.astype(bf16), wout_ref[...])
    o_ref[...] = _layer_norm(DEEPNORM_ALPHA * x_ref[...] + y, lng_ref[...], lnb_ref[...])


def _merge(x, cols, h_ml, h_rw, h_gd, b_gates, w_ml, w_rw, w_gd, w_out, ln_g, ln_b):
    t, d = x.shape
    tm = _pick(t, (512, 384, 128))
    tok = lambda n: pl.BlockSpec((tm, n), lambda i: (i, 0))
    full = lambda a: pl.BlockSpec(a.shape, lambda i: (0, 0))
    return pl.pallas_call(
        _merge_body,
        grid=(t // tm,),
        in_specs=[tok(d), pl.BlockSpec((tm, GATE_COLS), lambda i: (i, SEG_GATE // GATE_COLS)),
                  tok(ML_W), tok(RW_W), tok(GD_W),
                  full(b_gates), full(w_ml), full(w_rw), full(w_gd), full(w_out), full(ln_g), full(ln_b)],
        out_specs=tok(d),
        out_shape=jax.ShapeDtypeStruct((t, d), f32),
        compiler_params=_params(1),
        name="merge",
    )(x, cols, h_ml, h_rw, h_gd, b_gates, w_ml, w_rw, w_gd, w_out, ln_g, ln_b)


def _route(lt):
    grp = [lt[i:i + 1, :] for i in range(N_GROUPS)]
    gmax = functools.reduce(jnp.maximum, grp)
    gex = [jnp.exp(g - gmax) for g in grp]
    gsum = functools.reduce(lambda a, b: a + b, gex)
    gp = [e / gsum for e in gex]
    p_sel = functools.reduce(jnp.maximum, gp)
    taken = jnp.zeros_like(p_sel)
    sel = []
    for i in range(N_GROUPS):
        s_i = jnp.where(gp[i] == p_sel, 1.0, 0.0) * (1.0 - taken)
        taken = taken + s_i
        sel.append(s_i)
    e_in = []
    for j in range(EXP_PER_GROUP):
        acc = sel[0] * lt[8 + j:9 + j, :]
        for g in range(1, N_GROUPS):
            acc = acc + sel[g] * lt[8 + g * EXP_PER_GROUP + j:9 + g * EXP_PER_GROUP + j, :]
        e_in.append(acc)
    emax = functools.reduce(jnp.maximum, e_in)
    eex = [jnp.exp(e - emax) for e in e_in]
    esum = functools.reduce(lambda a, b: a + b, eex)
    ep = [e / esum for e in eex]
    top = []
    for j in range(EXP_PER_GROUP):
        rank = jnp.zeros_like(p_sel)
        for i in range(EXP_PER_GROUP):
            if i == j:
                continue
            ahead = (ep[i] > ep[j]) if i > j else (ep[i] >= ep[j])
            rank = rank + jnp.where(ahead, 1.0, 0.0)
        top.append(jnp.where(rank < 2.0, ep[j], 0.0))
    tsum = functools.reduce(lambda a, b: a + b, top)
    wts = [tp / tsum * p_sel for tp in top]
    return [sel[g] * wts[j] for g in range(N_GROUPS) for j in range(EXP_PER_GROUP)]


def _ffn_body(x_ref, pe_ref, wr_ref, br_ref, wgu_ref, wd_ref, wpg_ref, bpg_ref, wp_ref, lng_ref, lnb_ref,
              o_ref, xb_s, gate_s, acc_s):
    e = pl.program_id(1)
    tm = x_ref.shape[0]

    @pl.when(e == 0)
    def _():
        x = x_ref[...]
        xb_s[...] = x.astype(bf16)
        logits = jnp.dot(x, wr_ref[...], precision=lax.Precision.HIGHEST, preferred_element_type=f32) + br_ref[...]
        rows = _route(logits.T)
        gt = jnp.concatenate(rows + [jnp.zeros((128 - N_EXPERTS, tm), f32)], axis=0)
        gate_s[...] = gt.T

    gw = EXP_PER_GROUP * D_EXPERT
    xb = xb_s[...]
    hgu = jnp.dot(xb, wgu_ref[0], preferred_element_type=f32)
    sel_r = lax.broadcasted_iota(jnp.int32, (128, gw), 0)
    sel_c = lax.broadcasted_iota(jnp.int32, (128, gw), 1)
    sel = (sel_r == e * EXP_PER_GROUP + (sel_c >> D_EXPERT_SHIFT)).astype(bf16)
    gate_wide = _cumsum_lanes(gate_s[...], sel)
    hidden = (_silu(hgu[:, :gw]) * hgu[:, gw:] * gate_wide).astype(bf16)
    contrib = jnp.dot(hidden, wd_ref[0], preferred_element_type=f32)

    @pl.when(e == 0)
    def _():
        acc_s[...] = contrib

    @pl.when(e > 0)
    def _():
        acc_s[...] += contrib

    @pl.when(e == N_GROUPS - 1)
    def _():
        ple = (jax.nn.sigmoid(jnp.dot(xb, wpg_ref[...], preferred_element_type=f32) + bpg_ref[...])
               * jnp.dot(pe_ref[...].astype(bf16), wp_ref[...], preferred_element_type=f32))
        o_ref[...] = _layer_norm(DEEPNORM_ALPHA * x_ref[...] + acc_s[...] + ple, lng_ref[...], lnb_ref[...])


def _ffn(x, pe, w_r, b_r, e_gate, e_up, e_down, w_pg, b_pg, w_p, ln_g, ln_b):
    t, d = x.shape
    gw = EXP_PER_GROUP * D_EXPERT
    by_group = lambda w: w.reshape(N_GROUPS, EXP_PER_GROUP, d, D_EXPERT).transpose(0, 2, 1, 3).reshape(N_GROUPS, d, gw)
    w_gu = jnp.concatenate([by_group(e_gate), by_group(e_up)], axis=-1).astype(bf16)
    w_d = e_down.reshape(N_GROUPS, gw, d).astype(bf16)
    tm = _pick(t, (768, 384, 128))
    tok = lambda n: pl.BlockSpec((tm, n), lambda i, e: (i, 0))
    full = lambda a: pl.BlockSpec(a.shape, lambda i, e: (0,) * a.ndim)
    return pl.pallas_call(
        _ffn_body,
        grid=(t // tm, N_GROUPS),
        in_specs=[tok(d), tok(D_PLE), full(w_r), full(b_r),
                  pl.BlockSpec((1, d, 2 * gw), lambda i, e: (e, 0, 0)),
                  pl.BlockSpec((1, gw, d), lambda i, e: (e, 0, 0)),
                  full(w_pg), full(b_pg), full(w_p), full(ln_g), full(ln_b)],
        out_specs=tok(d),
        out_shape=jax.ShapeDtypeStruct((t, d), f32),
        scratch_shapes=[pltpu.VMEM((tm, d), bf16), pltpu.VMEM((tm, 128), f32), pltpu.VMEM((tm, d), f32)],
        compiler_params=_params(2),
        name="ffn",
    )(x, pe, w_r, b_r, w_gu, w_d, w_pg, b_pg, w_p, ln_g, ln_b)


def _lane_vec(n, off, vals):
    return jnp.zeros((1, n), f32).at[0, off:off + vals.shape[0]].set(vals)


def _layer(x, pe, st, lw, ncp, ns):
    ml_c, ml_n, ml_m, rw_s, rw_shift, gd_s, gd_conv = st
    w = lw['w_in']
    d = w.shape[0]
    z = lambda n: jnp.zeros((d, n), f32)
    w_proj = jnp.concatenate([
        w[:, 0:ML_MAIN],
        w[:, ML_MAIN:ML_COLS], w[:, ML_COLS + RW_COLS + GD_MAIN:ML_COLS + RW_COLS + GD_COLS], z(SMALL_W - 16),
        w[:, ML_COLS:ML_COLS + RW_COLS], z(SEG_GD - SEG_RW - RW_COLS),
        w[:, ML_COLS + RW_COLS:ML_COLS + RW_COLS + GD_MAIN],
        w[:, ML_COLS + RW_COLS + GD_COLS:],
    ], axis=1).astype(bf16)
    cols = _inproj(x, w_proj)

    gate_bias = jnp.concatenate([lw['ml_i_bias'], lw['ml_f_bias']])
    cn0 = jnp.concatenate([ml_c, ml_n[..., None], jnp.zeros(ml_c.shape[:-1] + (ML_DV - 1,), f32)], axis=-1)
    h_ml, cn1, m1 = _mlstm(cols, _lane_vec(128, 0, gate_bias), _lane_vec(16, 0, gate_bias).T,
                           lw['ml_norm'][None, :], cn0, ml_m[..., None, None], ncp, ns)
    row = lambda a: a.reshape(1, -1)
    shift0 = jnp.concatenate([jnp.zeros((ns, 7, RW_COLS), f32), rw_shift], axis=1)
    h_rw, rws1, shift1 = _rwkv(cols, row(lw['rw_mu']), row(lw['rw_w0']), lw['rw_w2'].astype(bf16), row(lw['rw_a0']),
                               lw['rw_a2'].astype(bf16), lw['rw_g2'].astype(bf16), row(lw['rw_k_k']), row(lw['rw_k_a']),
                               row(lw['rw_r_k']), row(lw['rw_gn_g']), row(lw['rw_gn_b']), shift0, rw_s, ncp, ns)
    conv0 = jnp.concatenate([jnp.zeros((ns, 8 - (GD_CONV - 1), GD_CONV_CH), f32), gd_conv], axis=1)
    convw = jnp.concatenate([lw['gd_conv_w'], jnp.zeros((8 - GD_CONV, GD_CONV_CH), f32)], axis=0)
    h_gd, gds1, conv1 = _gdn(cols, convw, _lane_vec(128, 12, lw['gd_a_log']), _lane_vec(16, 12, lw['gd_a_log']).T,
                             _lane_vec(128, 12, lw['gd_dt_bias']), _lane_vec(16, 12, lw['gd_dt_bias']).T,
                             lw['gd_norm'][None, :], conv0, gd_s, ncp, ns)

    x = _merge(x, cols, h_ml, h_rw, h_gd, lw['b_gates'].reshape(1, -1), lw['w_br_ml'].astype(bf16),
               lw['w_br_rw'].astype(bf16), lw['w_br_gd'].astype(bf16), lw['w_out'].astype(bf16),
               row(lw['ln1_g']), row(lw['ln1_b']))

    w_r = jnp.concatenate([lw['w_rg'], z(8 - N_GROUPS), lw['w_re'], z(128 - 8 - N_EXPERTS)], axis=1)
    b_r = jnp.concatenate([lw['b_rg'], jnp.zeros((8 - N_GROUPS,), f32), lw['b_re'],
                           jnp.zeros((128 - 8 - N_EXPERTS,), f32)])[None, :]
    x = _ffn(x, pe, w_r, b_r, lw['e_gate'], lw['e_up'], lw['e_down'],
             lw['ple_gate_w'].astype(bf16), row(lw['ple_gate_b']), lw['ple_w'].astype(bf16),
             row(lw['ln2_g']), row(lw['ln2_b']))

    new_st = (cn1[..., :ML_DV], cn1[..., ML_DV], m1[..., 0, 0], rws1, shift1[:, 7:8, :], gds1,
              conv1[:, 8 - (GD_CONV - 1):, :])
    return x, new_st


_WEIGHT_NAMES = ('w_in', 'ml_i_bias', 'ml_f_bias', 'ml_norm', 'rw_mu', 'rw_w0', 'rw_w2', 'rw_a0', 'rw_a2', 'rw_g2',
                 'rw_k_k', 'rw_k_a', 'rw_r_k', 'rw_gn_g', 'rw_gn_b', 'gd_conv_w', 'gd_a_log', 'gd_dt_bias', 'gd_norm',
                 'w_br_ml', 'w_br_rw', 'w_br_gd', 'b_gates', 'w_out', 'ln1_g', 'ln1_b', 'w_rg', 'b_rg', 'w_re', 'b_re',
                 'e_gate', 'e_up', 'e_down', 'ple_w', 'ple_gate_w', 'ple_gate_b', 'ln2_g', 'ln2_b')


def _trunk(x_prompt, x_sample, p_prompt, p_sample, states, weights):
    bp, seq, d = x_prompt.shape
    bs, dseq, _ = x_sample.shape
    assert bp == 1 and seq % CHUNK == 0 and dseq == CHUNK
    ncp, ns = seq // CHUNK, bs
    depth = p_prompt.shape[0]
    x = jnp.concatenate([x_prompt.reshape(seq, d), x_sample.reshape(bs * dseq, d)], axis=0)
    new_states = []
    for l in range(depth):
        pe = jnp.concatenate([p_prompt[l].reshape(seq, -1), p_sample[l].reshape(bs * dseq, -1)], axis=0)
        lw = {k: v[l] for k, v in weights.items()}
        x, st = _layer(x, pe, tuple(s[l] for s in states), lw, ncp, ns)
        new_states.append(st)
    y_p = x[:seq].reshape(bp, seq, d)
    y_s = x[seq:].reshape(bs, dseq, d)
    stacked = [jnp.stack([st[j] for st in new_states]) for j in range(7)]
    outs_p = [s[:, :1] for s in stacked]
    outs_s = [s[:, 1:] for s in stacked]
    return (y_p, y_s, *outs_p, *outs_s)


def kernel(x_prompt, x_sample, p_prompt, p_sample, state_mlstm_c, state_mlstm_n, state_mlstm_m, state_rwkv_s, state_rwkv_shift, state_gdn_s, state_gdn_conv, w_in, ml_i_bias, ml_f_bias, ml_norm, rw_mu, rw_w0, rw_w2, rw_a0, rw_a2, rw_g2, rw_k_k, rw_k_a, rw_r_k, rw_gn_g, rw_gn_b, gd_conv_w, gd_a_log, gd_dt_bias, gd_norm, w_br_ml, w_br_rw, w_br_gd, b_gates, w_out, ln1_g, ln1_b, w_rg, b_rg, w_re, b_re, e_gate, e_up, e_down, ple_w, ple_gate_w, ple_gate_b, ln2_g, ln2_b):
    states = (state_mlstm_c, state_mlstm_n, state_mlstm_m, state_rwkv_s, state_rwkv_shift, state_gdn_s, state_gdn_conv)
    wvals = (w_in, ml_i_bias, ml_f_bias, ml_norm, rw_mu, rw_w0, rw_w2, rw_a0, rw_a2, rw_g2, rw_k_k, rw_k_a, rw_r_k,
             rw_gn_g, rw_gn_b, gd_conv_w, gd_a_log, gd_dt_bias, gd_norm, w_br_ml, w_br_rw, w_br_gd, b_gates, w_out,
             ln1_g, ln1_b, w_rg, b_rg, w_re, b_re, e_gate, e_up, e_down, ple_w, ple_gate_w, ple_gate_b, ln2_g, ln2_b)
    return _trunk(x_prompt, x_sample, p_prompt, p_sample, states, dict(zip(_WEIGHT_NAMES, wvals)))
```

```python
import functools

import jax
import jax.numpy as jnp
from jax import lax
from jax.experimental import pallas as pl
from jax.experimental.pallas import tpu as pltpu

f32 = jnp.float32
bf16 = jnp.bfloat16

D_MODEL = 1024
DEPTH = 4
CHUNK = 64
CHUNK_SHIFT = 6
D_PLE = 256
ML_HEADS, ML_DQK, ML_DV = 4, 64, 128
ML_W = ML_HEADS * ML_DV
RW_HEADS, RW_DH = 8, 64
RW_W = RW_HEADS * RW_DH
RW_W_RANK, RW_A_RANK, RW_G_RANK = 64, 64, 128
GD_HEADS, GD_DK, GD_DV = 4, 128, 128
GD_W = GD_HEADS * GD_DV
GD_CONV = 4
N_GROUPS, EXP_PER_GROUP = 4, 4
N_EXPERTS = N_GROUPS * EXP_PER_GROUP
D_EXPERT = 256
ML_COLS = 2 * ML_HEADS * ML_DQK + 2 * ML_W + 2 * ML_HEADS
RW_COLS = 3 * RW_W + RW_W_RANK + RW_A_RANK + RW_G_RANK
GD_CONV_CH = 2 * GD_HEADS * GD_DK + GD_W
GD_COLS = GD_CONV_CH + GD_W + 2 * GD_HEADS
GATE_COLS = 3 * D_MODEL
DEEPNORM_ALPHA = (2 * DEPTH) ** 0.25
LN_EPS = 1e-5
GN_EPS = 64e-5
D_EXPERT_SHIFT = 8
assert CHUNK == 1 << CHUNK_SHIFT and D_EXPERT == 1 << D_EXPERT_SHIFT

ML_MAIN = 2 * ML_HEADS * ML_DQK + 2 * ML_W
SMALL_W = 256
GD_MAIN = GD_CONV_CH + GD_W
SEG_ML = 0
SEG_SMALL = ML_MAIN
SEG_RW = SEG_SMALL + SMALL_W
SEG_GD = 2 * GD_MAIN
SEG_GATE = SEG_GD + GD_MAIN
PROJ_COLS = SEG_GATE + GATE_COLS
assert SEG_RW == RW_COLS and SEG_GATE == 2 * GATE_COLS and SEG_GD >= SEG_RW + RW_COLS

VMEM_LIMIT = 56 * 1024 * 1024
MAX_CHUNKS_PER_STEP = 4

_NN = (((1,), (0,)), ((), ()))
_NT = (((1,), (1,)), ((), ()))
_TN = (((0,), (0,)), ((), ()))


def _dot(a, b, dims=_NN):
    return lax.dot_general(a, b, dims, preferred_element_type=f32)


def _mm(a, b, dims=_NN):
    return _dot(a.astype(bf16), b.astype(bf16), dims)


def _split3(x):
    h1 = x.astype(bf16)
    r1 = x - h1.astype(f32)
    h2 = r1.astype(bf16)
    h3 = (r1 - h2.astype(f32)).astype(bf16)
    return h1, h2, h3


def _cumsum_rows(tri, x):
    h1, h2, h3 = _split3(x)
    return _dot(tri, h1) + (_dot(tri, h2) + _dot(tri, h3))


def _cumsum_lanes(x, tri_t):
    h1, h2, h3 = _split3(x)
    return _dot(h1, tri_t) + (_dot(h2, tri_t) + _dot(h3, tri_t))


def _softplus(x):
    return jnp.maximum(x, 0.0) + jnp.log1p(jnp.exp(-jnp.abs(x)))


def _log_sigmoid(x):
    return -_softplus(-x)


def _silu(x):
    return x * jax.nn.sigmoid(x)


def _iotas(n):
    r = lax.broadcasted_iota(jnp.int32, (n, n), 0)
    c = lax.broadcasted_iota(jnp.int32, (n, n), 1)
    return r, c


def _chunk_tris(nr):
    r, c = _iotas(nr)
    same = (r >> CHUNK_SHIFT) == (c >> CHUNK_SHIFT)
    return (same & (r >= c)).astype(bf16), (same & (r <= c)).astype(bf16)


def _tri_inv_many(mats, r, c):
    eye = (r == c).astype(f32)
    blk8 = (r >> 3) == (c >> 3)
    ns = [jnp.where(blk8, a, 0.0) for a in mats]
    nb = [n.astype(bf16) for n in ns]
    n2 = [_dot(p, p) for p in nb]
    n2b = [x.astype(bf16) for x in n2]
    n3 = [_dot(p, q) for p, q in zip(nb, n2b)]
    n4 = [_dot(q, q) for q in n2b]
    ps = [eye - n + m2 - m3 for n, m2, m3 in zip(ns, n2, n3)]
    xs = [p + _dot(p.astype(bf16), m4.astype(bf16)) for p, m4 in zip(ps, n4)]
    for sh in (3, 4, 5):
        off = ((r >> (sh + 1)) == (c >> (sh + 1))) & ((r >> sh) != (c >> sh))
        xb = [x.astype(bf16) for x in xs]
        ts = [_dot(jnp.where(off, a, 0.0).astype(bf16), q) for a, q in zip(mats, xb)]
        xs = [x - _dot(q, t.astype(bf16)) for x, q, t in zip(xs, xb, ts)]
    return xs


def _layer_norm(x, g, b):
    mu = jnp.mean(x, axis=-1, keepdims=True)
    xc = x - mu
    var = jnp.mean(xc * xc, axis=-1, keepdims=True)
    return xc * lax.rsqrt(var + LN_EPS) * g + b


def _params(n_axes):
    return pltpu.CompilerParams(dimension_semantics=("arbitrary",) * n_axes, vmem_limit_bytes=VMEM_LIMIT)


def _pick(n, cands):
    for t in cands:
        if n % t == 0:
            return t
    raise ValueError(f"no tile for {n} in {cands}")


def _chunks_per_step(n):
    return _pick(n, tuple(k for k in (4, 2, 1) if k <= MAX_CHUNKS_PER_STEP))


def _inproj_body(x_ref, w_ref, o_ref, xb_s):
    @pl.when(pl.program_id(1) == 0)
    def _():
        xb_s[...] = x_ref[...].astype(bf16)

    o_ref[...] = jnp.dot(xb_s[...], w_ref[...], preferred_element_type=f32)


def _inproj(x, w):
    t, k = x.shape
    n = w.shape[1]
    tm = _pick(t, (1536, 768, 512, 384, 128))
    tn = _pick(n, (1024, 512))
    return pl.pallas_call(
        _inproj_body,
        grid=(t // tm, n // tn),
        in_specs=[pl.BlockSpec((tm, k), lambda i, j: (i, 0)), pl.BlockSpec((k, tn), lambda i, j: (0, j))],
        out_specs=pl.BlockSpec((tm, tn), lambda i, j: (i, j)),
        out_shape=jax.ShapeDtypeStruct((t, n), f32),
        scratch_shapes=[pltpu.VMEM((tm, k), bf16)],
        compiler_params=_params(2),
        name="inproj",
    )(x, w)


def _mixer_call(body, name, cols, col_specs, params, states, state_shapes, out_width, n_chunks, chunk_off):
    carry = states is None
    nch = _chunks_per_step(n_chunks)
    nr = nch * CHUNK
    assert (chunk_off * CHUNK) % nr == 0
    off = chunk_off * CHUNK // nr
    nz = lambda shp: (0,) * len(shp)
    in_specs = [pl.BlockSpec((nr, w), lambda i, cb=cb: (off + i, cb)) for w, cb in col_specs]
    in_specs += [pl.BlockSpec(p.shape, lambda i, k=p.ndim: (0,) * k) for p in params]
    args = [cols] * len(col_specs) + list(params)
    if carry:
        st_specs = [pl.BlockSpec((1,) + shp, lambda i, shp=shp: (0,) + nz(shp)) for shp in state_shapes]
        n_st = 1
        scratch = [pltpu.VMEM(shp, f32) for shp in state_shapes]
    else:
        st_specs = [pl.BlockSpec((nch,) + shp, lambda i, shp=shp: (i,) + nz(shp)) for shp in state_shapes]
        in_specs += st_specs
        args += list(states)
        n_st = n_chunks
        scratch = []
    return pl.pallas_call(
        functools.partial(body, nch=nch, carry=carry),
        grid=(n_chunks // nch,),
        in_specs=in_specs,
        out_specs=[pl.BlockSpec((nr, out_width), lambda i: (i, 0))] + st_specs,
        out_shape=[jax.ShapeDtypeStruct((n_chunks * CHUNK, out_width), bf16)]
        + [jax.ShapeDtypeStruct((n_st,) + shp, f32) for shp in state_shapes],
        scratch_shapes=scratch,
        compiler_params=_params(1),
        name=name + ("_seq" if carry else "_chunks"),
    )(*args)


def _run_mixer(body, name, cols, col_specs, params, states, state_shapes, out_width, ncp, ns):
    outs_p = _mixer_call(body, name, cols, col_specs, params, None, state_shapes, out_width, ncp, 0)
    outs_s = _mixer_call(body, name, cols, col_specs, params, states, state_shapes, out_width, ns, ncp)
    return [(outs_p[0], outs_s[0])] + [jnp.concatenate([a, b], axis=0) for a, b in zip(outs_p[1:], outs_s[1:])]


def _init_carried(carry, scratch_refs):
    if carry:
        @pl.when(pl.program_id(0) == 0)
        def _():
            for s in scratch_refs:
                s[...] = jnp.zeros_like(s)


def _shifted_rows(x, head_rows, nch, carry, shift):
    L = CHUNK
    if carry:
        full = jnp.concatenate([head_rows(0), x], axis=0)
        return full[8 - shift:8 - shift + x.shape[0]]
    parts = [jnp.concatenate([head_rows(j), x[j * L:(j + 1) * L]], axis=0)[8 - shift:8 - shift + L] for j in range(nch)]
    return jnp.concatenate(parts, axis=0) if nch > 1 else parts[0]


def _mlstm_body(*refs, nch, carry):
    if carry:
        main_ref, sm_ref, brow_ref, bcol_ref, norm_ref, h_ref, cn_out_ref, m_out_ref, cn_s, m_s = refs
    else:
        main_ref, sm_ref, brow_ref, bcol_ref, norm_ref, cn0_ref, m0_ref, h_ref, cn_out_ref, m_out_ref = refs
        cn_s = m_s = None
    L = CHUNK
    nr = nch * L
    _init_carried(carry, (cn_s, m_s))
    hs, js = range(ML_HEADS), range(nch)
    rows = [slice(j * L, (j + 1) * L) for j in js]

    r, cc = _iotas(L)
    causal = r >= cc
    tri, tri_t = _chunk_tris(nr)
    one_lane0 = (lax.broadcasted_iota(jnp.int32, (L, ML_DV), 1) == 0).astype(f32)

    sm = sm_ref[:, :128]
    g_c = sm + brow_ref[...]
    g_r = sm.T[:16, :] + bcol_ref[...]
    b_c = _cumsum_rows(tri, _log_sigmoid(g_c))
    b_r = _cumsum_lanes(_log_sigmoid(g_r), tri_t)
    cn_in = [cn_s[h] for h in hs] if carry else None
    m_in = [m_s[h] for h in hs] if carry else None

    bc = [[b_c[rows[j], 4 + h:5 + h] for h in hs] for j in js]
    d_intra = [[jnp.where(causal, bc[j][h] - b_r[4 + h:5 + h, rows[j]] + g_r[h:h + 1, rows[j]], -jnp.inf)
                for h in hs] for j in js]
    d_max = [[jnp.max(d_intra[j][h], axis=-1, keepdims=True) for h in hs] for j in js]
    b_last = [[bc[j][h][L - 1:L, :] for h in hs] for j in js]
    d_last = [[b_last[j][h] - bc[j][h] + g_c[rows[j], h:h + 1] for h in hs] for j in js]
    dl_max = [[jnp.max(d_last[j][h], axis=0, keepdims=True) for h in hs] for j in js]
    m_prev, m_new = [], []
    for j in js:
        m_prev.append((m_in if j == 0 else m_new[j - 1]) if carry else [m0_ref[j, h] for h in hs])
        m_new.append([jnp.maximum(b_last[j][h] + m_prev[j][h], dl_max[j][h]) for h in hs])
    m_t = [[jnp.maximum(bc[j][h] + m_prev[j][h], d_max[j][h]) for h in hs] for j in js]

    qb = [[(main_ref[rows[j], h * ML_DQK:(h + 1) * ML_DQK] * (ML_DQK ** -0.5)).astype(bf16) for h in hs] for j in js]
    kb = [[main_ref[rows[j], 256 + h * ML_DQK:256 + (h + 1) * ML_DQK].astype(bf16) for h in hs] for j in js]
    v = [[main_ref[rows[j], 512 + h * ML_DV:512 + (h + 1) * ML_DV] for h in hs] for j in js]
    qk = [[_dot(qb[j][h], kb[j][h], _NT) for h in hs] for j in js]
    kv = [[_dot(kb[j][h], (jnp.concatenate([v[j][h], one_lane0], axis=-1)
                           * jnp.exp(d_last[j][h] - m_new[j][h])).astype(bf16), _TN) for h in hs] for j in js]
    s = [[qk[j][h] * jnp.exp(d_intra[j][h] - m_t[j][h]) for h in hs] for j in js]
    sv = [[_mm(s[j][h], v[j][h]) for h in hs] for j in js]
    cn, cn_new = [], []
    for j in js:
        cn.append((cn_in if j == 0 else cn_new[j - 1]) if carry else [cn0_ref[j, h] for h in hs])
        cn_new.append([jnp.exp(b_last[j][h] + m_prev[j][h] - m_new[j][h]) * cn[j][h] + kv[j][h] for h in hs])
    qcn = [[_dot(qb[j][h], cn[j][h].astype(bf16)) for h in hs] for j in js]

    o_gate = jax.nn.sigmoid(main_ref[:, 1024:1024 + ML_W]) * norm_ref[...]
    jh = [(j, h) for j in js for h in hs]
    s_sum = [jnp.sum(s[j][h], axis=-1, keepdims=True) for j, h in jh]
    w_inter = [jnp.exp(bc[j][h] + m_prev[j][h] - m_t[j][h]) for j, h in jh]
    r_den = [1.0 / jnp.maximum(jnp.abs(w_inter[i] * qcn[j][h][:, ML_DV:ML_DV + 1] + s_sum[i]), jnp.exp(-m_t[j][h]))
             for i, (j, h) in enumerate(jh)]
    hh = [(w_inter[i] * qcn[j][h][:, :ML_DV] + sv[j][h]) * r_den[i] for i, (j, h) in enumerate(jh)]
    ms = [jnp.mean(x_ * x_, axis=-1, keepdims=True) for x_ in hh]
    outs = [(hh[i] * lax.rsqrt(ms[i] + 1e-6) * o_gate[rows[j], h * ML_DV:(h + 1) * ML_DV]).astype(h_ref.dtype)
            for i, (j, h) in enumerate(jh)]

    for j in js:
        for h in hs:
            h_ref[rows[j], h * ML_DV:(h + 1) * ML_DV] = outs[j * ML_HEADS + h]
    for h in hs:
        if carry:
            cn_s[h] = cn_new[nch - 1][h]
            m_s[h] = m_new[nch - 1][h]
            cn_out_ref[0, h] = cn_new[nch - 1][h]
            m_out_ref[0, h] = m_new[nch - 1][h]
        else:
            for j in js:
                cn_out_ref[j, h] = cn_new[j][h]
                m_out_ref[j, h] = m_new[j][h]


def _mlstm(cols, brow, bcol, norm, cn0, m0, ncp, ns):
    return _run_mixer(_mlstm_body, "mlstm", cols, [(ML_MAIN, SEG_ML // ML_MAIN), (SMALL_W, SEG_SMALL // SMALL_W)],
                      [brow, bcol, norm], [cn0, m0], [(ML_HEADS, ML_DQK, 2 * ML_DV), (ML_HEADS, 1, 1)], ML_W, ncp, ns)


def _gdn_body(*refs, nch, carry):
    if carry:
        (main_ref, sm_ref, convw_ref, arow_ref, acol_ref, dtrow_ref, dtcol_ref, norm_ref,
         o_ref, s_out_ref, conv_out_ref, s_s, prev_s) = refs
    else:
        (main_ref, sm_ref, convw_ref, arow_ref, acol_ref, dtrow_ref, dtcol_ref, norm_ref, s0_ref, conv0_ref,
         o_ref, s_out_ref, conv_out_ref) = refs
        s_s = prev_s = None
    L = CHUNK
    nr = nch * L
    _init_carried(carry, (s_s, prev_s))
    hs, js = range(GD_HEADS), range(nch)
    rows = [slice(j * L, (j + 1) * L) for j in js]

    r, cc = _iotas(L)
    incl = r >= cc
    strict = r > cc
    tri, tri_t = _chunk_tris(nr)

    x = main_ref[:, :GD_CONV_CH]
    s_in = [s_s[h] for h in hs] if carry else None
    head_rows = (lambda j: prev_s[...]) if carry else (lambda j: conv0_ref[j])
    conv = convw_ref[3:4, :] * x
    for d in (1, 2, 3):
        conv = conv + convw_ref[3 - d:4 - d, :] * _shifted_rows(x, head_rows, nch, carry, d)
    conv = _silu(conv)

    sm = sm_ref[:, :128]
    beta_all = jax.nn.sigmoid(sm)
    la_c = -jnp.exp(arow_ref[...]) * _softplus(sm + dtrow_ref[...])
    la_r = -jnp.exp(acol_ref[...]) * _softplus(sm.T[:16, :] + dtcol_ref[...])
    gc_c = _cumsum_rows(tri, la_c)
    gc_r = _cumsum_lanes(la_r, tri_t)

    q, k, v, beta, gcc, dec = ([[None] * GD_HEADS for _ in js] for _ in range(6))
    q_raw = [[conv[rows[j], h * GD_DK:(h + 1) * GD_DK] for h in hs] for j in js]
    k_raw = [[conv[rows[j], 512 + h * GD_DK:512 + (h + 1) * GD_DK] for h in hs] for j in js]
    q_ss = [[jnp.sum(q_raw[j][h] * q_raw[j][h], axis=-1, keepdims=True) for h in hs] for j in js]
    k_ss = [[jnp.sum(k_raw[j][h] * k_raw[j][h], axis=-1, keepdims=True) for h in hs] for j in js]
    for j in js:
        for h in hs:
            q[j][h] = q_raw[j][h] * (lax.rsqrt(q_ss[j][h] + 1e-6) * (GD_DK ** -0.5))
            k[j][h] = k_raw[j][h] * lax.rsqrt(k_ss[j][h] + 1e-6)
            v[j][h] = conv[rows[j], 1024 + h * GD_DV:1024 + (h + 1) * GD_DV]
            beta[j][h] = beta_all[rows[j], 8 + h:9 + h]
            gcc[j][h] = gc_c[rows[j], 12 + h:13 + h]
            dec[j][h] = jnp.exp(jnp.where(incl, gcc[j][h] - gc_r[12 + h:13 + h, rows[j]], -jnp.inf))
    jh = [(j, h) for j in js for h in hs]
    kb = {p: k[p[0]][p[1]].astype(bf16) for p in jh}
    kk = {p: _dot(kb[p], kb[p], _NT) for p in jh}
    qk = {p: (_dot(q[p[0]][p[1]].astype(bf16), kb[p], _NT) * dec[p[0]][p[1]]).astype(bf16) for p in jh}
    a_mat = [jnp.where(strict, beta[j][h] * kk[(j, h)] * dec[j][h], 0.0) for j, h in jh]
    rhs = {p: jnp.concatenate([v[p[0]][p[1]] * beta[p[0]][p[1]],
                               k[p[0]][p[1]] * (beta[p[0]][p[1]] * jnp.exp(gcc[p[0]][p[1]]))], axis=-1) for p in jh}
    tinv = dict(zip(jh, _tri_inv_many(a_mat, r, cc)))
    sol = {p: _mm(tinv[p], rhs[p]) for p in jh}
    qg = {p: (q[p[0]][p[1]] * jnp.exp(gcc[p[0]][p[1]])).astype(bf16) for p in jh}
    g_last = {p: gcc[p[0]][p[1]][L - 1:L, :] for p in jh}
    kd = {p: (k[p[0]][p[1]] * jnp.exp(g_last[p] - gcc[p[0]][p[1]])).astype(bf16) for p in jh}

    z_gate = _silu(main_ref[:, GD_CONV_CH:])
    s_new = []
    o_all = []
    for j in js:
        s_cur = (s_in if j == 0 else s_new[j - 1]) if carry else [s0_ref[j, h] for h in hs]
        sb = [x_.astype(bf16) for x_ in s_cur]
        ub = [(sol[(j, h)][:, :GD_DV] - _dot(sol[(j, h)][:, GD_DV:].astype(bf16), sb[h])).astype(bf16) for h in hs]
        s_new.append([jnp.exp(g_last[(j, h)]) * s_cur[h] + _dot(kd[(j, h)], ub[h], _TN) for h in hs])
        o_all.extend(_dot(qg[(j, h)], sb[h]) + _dot(qk[(j, h)], ub[h]) for h in hs)
    o_ms = [jnp.mean(x_ * x_, axis=-1, keepdims=True) for x_ in o_all]
    outs = [(o_all[i] * lax.rsqrt(o_ms[i] + 1e-6) * norm_ref[...] * z_gate[rows[j], h * GD_DV:(h + 1) * GD_DV])
            .astype(o_ref.dtype) for i, (j, h) in enumerate(jh)]

    for j in js:
        for h in hs:
            o_ref[rows[j], h * GD_DV:(h + 1) * GD_DV] = outs[j * GD_HEADS + h]
    if carry:
        tail = x[nr - 8:, :]
        prev_s[...] = tail
        conv_out_ref[0] = tail
        for h in hs:
            s_s[h] = s_new[nch - 1][h]
            s_out_ref[0, h] = s_new[nch - 1][h]
    else:
        for j in js:
            conv_out_ref[j] = x[(j + 1) * L - 8:(j + 1) * L, :]
            for h in hs:
                s_out_ref[j, h] = s_new[j][h]


def _gdn(cols, convw, arow, acol, dtrow, dtcol, norm, conv0, s0, ncp, ns):
    return _run_mixer(_gdn_body, "gdn", cols, [(GD_MAIN, SEG_GD // GD_MAIN), (SMALL_W, SEG_SMALL // SMALL_W)],
                      [convw, arow, acol, dtrow, dtcol, norm], [s0, conv0],
                      [(GD_HEADS, GD_DK, GD_DV), (8, GD_CONV_CH)], GD_W, ncp, ns)


def _rwkv_body(*refs, nch, carry):
    if carry:
        (rw_ref, mu_ref, w0_ref, w2_ref, a0_ref, a2_ref, g2_ref, kk_ref, ka_ref, rk_ref, gng_ref, gnb_ref,
         y_ref, s_out_ref, shift_out_ref, s_s, prev_s) = refs
    else:
        (rw_ref, mu_ref, w0_ref, w2_ref, a0_ref, a2_ref, g2_ref, kk_ref, ka_ref, rk_ref, gng_ref, gnb_ref,
         s0_ref, shift0_ref, y_ref, s_out_ref, shift_out_ref) = refs
        s_s = prev_s = None
    L = CHUNK
    nr = nch * L
    dh = RW_DH
    _init_carried(carry, (s_s, prev_s))
    hs, js = range(RW_HEADS), range(nch)
    rows = [slice(j * L, (j + 1) * L) for j in js]
    sls = [slice(h * dh, (h + 1) * dh) for h in hs]

    r_i, c_i = _iotas(L)
    incl = r_i >= c_i
    strict = r_i > c_i
    tri, _ = _chunk_tris(nr)

    x = rw_ref[...]
    s_in = [s_s[h] for h in hs] if carry else None
    head_rows = (lambda j: prev_s[...]) if carry else (lambda j: shift0_ref[j])
    prev = _shifted_rows(x, head_rows, nch, carry, 1)
    mixed = x + (prev - x) * mu_ref[...]
    r_all = mixed[:, 0:RW_W]
    k_all = mixed[:, RW_W:2 * RW_W]
    v_all = mixed[:, 2 * RW_W:3 * RW_W]
    w_lo = mixed[:, 3 * RW_W:3 * RW_W + RW_W_RANK]
    a_lo = mixed[:, 3 * RW_W + RW_W_RANK:3 * RW_W + RW_W_RANK + RW_A_RANK]
    g_lo = mixed[:, 3 * RW_W + RW_W_RANK + RW_A_RANK:]
    w_log = -_softplus(-(w0_ref[...] + _mm(jnp.tanh(w_lo), w2_ref[...]))) - 0.5
    lw = -jnp.exp(w_log)
    a_all = jax.nn.sigmoid(a0_ref[...] + _mm(a_lo, a2_ref[...]))
    g_all = _mm(jax.nn.sigmoid(g_lo), g2_ref[...])
    cum = _cumsum_rows(tri, lw)
    cum_ex = cum - lw

    seg_r = lax.broadcasted_iota(jnp.int32, (RW_W, 128), 0)
    seg_c = lax.broadcasted_iota(jnp.int32, (RW_W, 128), 1)
    seg = ((seg_r >> 6) == seg_c).astype(bf16)
    seg_tr = lax.broadcasted_iota(jnp.int32, (128, RW_W), 0)
    seg_tc = lax.broadcasted_iota(jnp.int32, (128, RW_W), 1)
    seg_t = (seg_tr == (seg_tc >> 6)).astype(bf16)
    head_sum = lambda a: _dot(a.astype(bf16), seg)
    head_bcast = lambda a: _dot(a.astype(bf16), seg_t)

    kk_all = k_all * kk_ref[...]
    kk_all = kk_all * head_bcast(lax.rsqrt(head_sum(kk_all * kk_all) + 1e-6))
    kmod_all = k_all * (1.0 + (a_all - 1.0) * ka_ref[...])
    e_neg = jnp.exp(-cum)
    rt_all = r_all * jnp.exp(cum)
    kt_all = kmod_all * e_neg
    at_all = -kk_all * jnp.exp(cum_ex)
    bt_all = kk_all * a_all * e_neg
    bonus_all = head_bcast(head_sum(r_all * kmod_all * rk_ref[...])) * v_all

    jh = [(j, h) for j in js for h in hs]
    piece = lambda a, p: a[rows[p[0]], sls[p[1]]]
    atb = {p: piece(at_all, p).astype(bf16) for p in jh}
    rtb = {p: piece(rt_all, p).astype(bf16) for p in jh}
    ktb = {p: piece(kt_all, p).astype(bf16) for p in jh}
    btb = {p: piece(bt_all, p).astype(bf16) for p in jh}
    vtb = {p: piece(v_all, p).T.astype(bf16) for p in jh}
    gmat = {p: _dot(jnp.concatenate([atb[p], rtb[p]], axis=0), jnp.concatenate([ktb[p], btb[p]], axis=0), _NT)
            for p in jh}
    mk = {p: jnp.concatenate([jnp.where(strict, gmat[p][:L, :L], 0.0), jnp.where(incl, gmat[p][L:, :L], 0.0)],
                             axis=0).astype(bf16) for p in jh}
    n_ab = [jnp.where(strict, -gmat[p][:L, L:], 0.0) for p in jh]
    n_rb = {p: jnp.where(incl, gmat[p][L:, L:], 0.0).astype(bf16) for p in jh}
    mkv_t = {p: _dot(vtb[p], mk[p], _NT) for p in jh}
    kv_s = {p: _dot(vtb[p], ktb[p]) for p in jh}
    tinv = {p: t.astype(bf16) for p, t in zip(jh, _tri_inv_many(n_ab, r_i, c_i))}
    tar = {p: jnp.concatenate([_dot(tinv[p], atb[p]).astype(bf16), rtb[p]], axis=0) for p in jh}
    m_t = {p: _dot(mkv_t[p][:, :L].astype(bf16), tinv[p], _NT) for p in jh}

    s_new = []
    y_rows = []
    for j in js:
        s_cur = (s_in if j == 0 else s_new[j - 1]) if carry else [s0_ref[j, h] for h in hs]
        st = [_dot(s_cur[h].astype(bf16), tar[(j, h)], _NT) for h in hs]
        ut = [(st[h][:, :L] + m_t[(j, h)]).astype(bf16) for h in hs]
        p_last = jnp.exp(cum[(j + 1) * L - 1:(j + 1) * L, :])
        s_new.append([(s_cur[h] + _dot(ut[h], btb[(j, h)]) + kv_s[(j, h)]) * p_last[:, sls[h]] for h in hs])
        yt = [st[h][:, L:] + mkv_t[(j, h)][:, L:] + _dot(ut[h], n_rb[(j, h)], _NT) for h in hs]
        y_rows.append(jnp.concatenate([yt[h].T for h in hs], axis=-1))

    y_all = jnp.concatenate(y_rows, axis=0) if nch > 1 else y_rows[0]
    mu_y = head_bcast(head_sum(y_all) * (1.0 / dh))
    yc = y_all - mu_y
    rstd = head_bcast(lax.rsqrt(head_sum(yc * yc) * (1.0 / dh) + GN_EPS))
    yn = yc * rstd * gng_ref[...] + gnb_ref[...] + bonus_all
    y_ref[...] = (yn * g_all).astype(y_ref.dtype)

    if carry:
        tail = x[nr - 8:, :]
        prev_s[...] = tail
        shift_out_ref[0] = tail
        for h in hs:
            s_s[h] = s_new[nch - 1][h]
            s_out_ref[0, h] = s_new[nch - 1][h]
    else:
        for j in js:
            shift_out_ref[j] = x[(j + 1) * L - 8:(j + 1) * L, :]
            for h in hs:
                s_out_ref[j, h] = s_new[j][h]


def _rwkv(cols, mu, w0, w2, a0, a2, g2, k_k, k_a, r_k, gn_g, gn_b, shift0, s0, ncp, ns):
    return _run_mixer(_rwkv_body, "rwkv", cols, [(RW_COLS, SEG_RW // RW_COLS)],
                      [mu, w0, w2, a0, a2, g2, k_k, k_a, r_k, gn_g, gn_b], [s0, shift0],
                      [(RW_HEADS, RW_DH, RW_DH), (8, RW_COLS)], RW_W, ncp, ns)


def _merge_body(x_ref, g_ref, hml_p, hml_s, hrw_p, hrw_s, hgd_p, hgd_s, bg_ref, wml_ref, wrw_ref, wgd_ref, wout_ref,
                lng_ref, lnb_ref, o_ref, *, n_prompt_tiles):
    d = D_MODEL
    dot = lambda a, b: jnp.dot(a, b, preferred_element_type=f32)
    in_prompt = pl.program_id(0) < n_prompt_tiles
    pick = lambda p_ref, s_ref: jnp.where(in_prompt, p_ref[...], s_ref[...])
    merged = (jax.nn.sigmoid(g_ref[:, 0:d] + bg_ref[:, 0:d]) * dot(pick(hml_p, hml_s), wml_ref[...])
              + jax.nn.sigmoid(g_ref[:, d:2 * d] + bg_ref[:, d:2 * d]) * dot(pick(hrw_p, hrw_s), wrw_ref[...])
              + jax.nn.sigmoid(g_ref[:, 2 * d:] + bg_ref[:, 2 * d:]) * dot(pick(hgd_p, hgd_s), wgd_ref[...]))
    y = dot(merged.astype(bf16), wout_ref[...])
    o_ref[...] = _layer_norm(DEEPNORM_ALPHA * x_ref[...] + y, lng_ref[...], lnb_ref[...])


def _merge(x, cols, h_ml, h_rw, h_gd, b_gates, w_ml, w_rw, w_gd, w_out, ln_g, ln_b):
    t, d = x.shape
    tp, ts = h_ml[0].shape[0], h_ml[1].shape[0]
    tm = _pick(ts, (512, 384, 128))
    assert tp % tm == 0 and tp + ts == t
    npt = tp // tm
    tok = lambda n: pl.BlockSpec((tm, n), lambda i: (i, 0))
    tok_p = lambda n: pl.BlockSpec((tm, n), lambda i: (jnp.minimum(i, npt - 1), 0))
    tok_s = lambda n: pl.BlockSpec((tm, n), lambda i: (jnp.maximum(i - npt, 0), 0))
    full = lambda a: pl.BlockSpec(a.shape, lambda i: (0, 0))
    return pl.pallas_call(
        functools.partial(_merge_body, n_prompt_tiles=npt),
        grid=(t // tm,),
        in_specs=[tok(d), pl.BlockSpec((tm, GATE_COLS), lambda i: (i, SEG_GATE // GATE_COLS)),
                  tok_p(ML_W), tok_s(ML_W), tok_p(RW_W), tok_s(RW_W), tok_p(GD_W), tok_s(GD_W),
                  full(b_gates), full(w_ml), full(w_rw), full(w_gd), full(w_out), full(ln_g), full(ln_b)],
        out_specs=tok(d),
        out_shape=jax.ShapeDtypeStruct((t, d), f32),
        compiler_params=_params(1),
        name="merge",
    )(x, cols, h_ml[0], h_ml[1], h_rw[0], h_rw[1], h_gd[0], h_gd[1], b_gates, w_ml, w_rw, w_gd, w_out, ln_g, ln_b)


def _route(lt):
    grp = [lt[i:i + 1, :] for i in range(N_GROUPS)]
    gmax = functools.reduce(jnp.maximum, grp)
    gex = [jnp.exp(g - gmax) for g in grp]
    gsum = functools.reduce(lambda a, b: a + b, gex)
    gp = [e / gsum for e in gex]
    p_sel = functools.reduce(jnp.maximum, gp)
    taken = jnp.zeros_like(p_sel)
    sel = []
    for i in range(N_GROUPS):
        s_i = jnp.where(gp[i] == p_sel, 1.0, 0.0) * (1.0 - taken)
        taken = taken + s_i
        sel.append(s_i)
    e_in = []
    for j in range(EXP_PER_GROUP):
        acc = sel[0] * lt[8 + j:9 + j, :]
        for g in range(1, N_GROUPS):
            acc = acc + sel[g] * lt[8 + g * EXP_PER_GROUP + j:9 + g * EXP_PER_GROUP + j, :]
        e_in.append(acc)
    emax = functools.reduce(jnp.maximum, e_in)
    eex = [jnp.exp(e - emax) for e in e_in]
    esum = functools.reduce(lambda a, b: a + b, eex)
    ep = [e / esum for e in eex]
    top = []
    for j in range(EXP_PER_GROUP):
        rank = jnp.zeros_like(p_sel)
        for i in range(EXP_PER_GROUP):
            if i == j:
                continue
            ahead = (ep[i] > ep[j]) if i > j else (ep[i] >= ep[j])
            rank = rank + jnp.where(ahead, 1.0, 0.0)
        top.append(jnp.where(rank < 2.0, ep[j], 0.0))
    tsum = functools.reduce(lambda a, b: a + b, top)
    wts = [tp / tsum * p_sel for tp in top]
    return [sel[g] * wts[j] for g in range(N_GROUPS) for j in range(EXP_PER_GROUP)]


def _ffn_body(x_ref, pe_ref, wr_ref, br_ref, wgu_ref, wd_ref, wpg_ref, bpg_ref, wp_ref, lng_ref, lnb_ref,
              o_ref, xb_s, gate_s, acc_s):
    e = pl.program_id(1)
    tm = x_ref.shape[0]

    @pl.when(e == 0)
    def _():
        x = x_ref[...]
        xb_s[...] = x.astype(bf16)
        logits = jnp.dot(x, wr_ref[...], precision=lax.Precision.HIGHEST, preferred_element_type=f32) + br_ref[...]
        rows = _route(logits.T)
        pad = jnp.zeros((128 - EXP_PER_GROUP, tm), f32)
        for g in range(N_GROUPS):
            gate_s[g] = jnp.concatenate(rows[g * EXP_PER_GROUP:(g + 1) * EXP_PER_GROUP] + [pad], axis=0).T

    gw = EXP_PER_GROUP * D_EXPERT
    xb = xb_s[...]
    hgu = jnp.dot(xb, wgu_ref[0], preferred_element_type=f32)
    gates = gate_s[e]
    gate_wide = jnp.concatenate([jnp.broadcast_to(gates[:, k:k + 1], (tm, D_EXPERT)) for k in range(EXP_PER_GROUP)],
                                axis=-1)
    hidden = (_silu(hgu[:, :gw]) * hgu[:, gw:] * gate_wide).astype(bf16)
    contrib = jnp.dot(hidden, wd_ref[0], preferred_element_type=f32)

    @pl.when(e == 0)
    def _():
        acc_s[...] = contrib

    @pl.when(e > 0)
    def _():
        acc_s[...] += contrib

    @pl.when(e == N_GROUPS - 1)
    def _():
        ple = (jax.nn.sigmoid(jnp.dot(xb, wpg_ref[...], preferred_element_type=f32) + bpg_ref[...])
               * jnp.dot(pe_ref[...].astype(bf16), wp_ref[...], preferred_element_type=f32))
        o_ref[...] = _layer_norm(DEEPNORM_ALPHA * x_ref[...] + acc_s[...] + ple, lng_ref[...], lnb_ref[...])


def _ffn(x, pe, w_r, b_r, e_gate, e_up, e_down, w_pg, b_pg, w_p, ln_g, ln_b):
    t, d = x.shape
    gw = EXP_PER_GROUP * D_EXPERT
    by_group = lambda w: w.reshape(N_GROUPS, EXP_PER_GROUP, d, D_EXPERT).transpose(0, 2, 1, 3).reshape(N_GROUPS, d, gw)
    w_gu = jnp.concatenate([by_group(e_gate), by_group(e_up)], axis=-1).astype(bf16)
    w_d = e_down.reshape(N_GROUPS, gw, d).astype(bf16)
    tm = _pick(t, (768, 384, 128))
    tok = lambda n: pl.BlockSpec((tm, n), lambda i, e: (i, 0))
    full = lambda a: pl.BlockSpec(a.shape, lambda i, e: (0,) * a.ndim)
    return pl.pallas_call(
        _ffn_body,
        grid=(t // tm, N_GROUPS),
        in_specs=[tok(d), tok(D_PLE), full(w_r), full(b_r),
                  pl.BlockSpec((1, d, 2 * gw), lambda i, e: (e, 0, 0)),
                  pl.BlockSpec((1, gw, d), lambda i, e: (e, 0, 0)),
                  full(w_pg), full(b_pg), full(w_p), full(ln_g), full(ln_b)],
        out_specs=tok(d),
        out_shape=jax.ShapeDtypeStruct((t, d), f32),
        scratch_shapes=[pltpu.VMEM((tm, d), bf16), pltpu.VMEM((N_GROUPS, tm, 128), f32), pltpu.VMEM((tm, d), f32)],
        compiler_params=_params(2),
        name="ffn",
    )(x, pe, w_r, b_r, w_gu, w_d, w_pg, b_pg, w_p, ln_g, ln_b)


def _lane_vec(n, off, vals):
    return jnp.zeros((1, n), f32).at[0, off:off + vals.shape[0]].set(vals)


def _layer(x, pe, st, lw, ncp, ns):
    ml_c, ml_n, ml_m, rw_s, rw_shift, gd_s, gd_conv = st
    w = lw['w_in']
    d = w.shape[0]
    z = lambda n: jnp.zeros((d, n), f32)
    w_proj = jnp.concatenate([
        w[:, 0:ML_MAIN],
        w[:, ML_MAIN:ML_COLS], w[:, ML_COLS + RW_COLS + GD_MAIN:ML_COLS + RW_COLS + GD_COLS], z(SMALL_W - 16),
        w[:, ML_COLS:ML_COLS + RW_COLS], z(SEG_GD - SEG_RW - RW_COLS),
        w[:, ML_COLS + RW_COLS:ML_COLS + RW_COLS + GD_MAIN],
        w[:, ML_COLS + RW_COLS + GD_COLS:],
    ], axis=1).astype(bf16)
    cols = _inproj(x, w_proj)

    gate_bias = jnp.concatenate([lw['ml_i_bias'], lw['ml_f_bias']])
    cn0 = jnp.concatenate([ml_c, ml_n[..., None], jnp.zeros(ml_c.shape[:-1] + (ML_DV - 1,), f32)], axis=-1)
    h_ml, cn1, m1 = _mlstm(cols, _lane_vec(128, 0, gate_bias), _lane_vec(16, 0, gate_bias).T,
                           lw['ml_norm'][None, :], cn0, ml_m[..., None, None], ncp, ns)
    row = lambda a: a.reshape(1, -1)
    shift0 = jnp.concatenate([jnp.zeros((ns, 7, RW_COLS), f32), rw_shift], axis=1)
    h_rw, rws1, shift1 = _rwkv(cols, row(lw['rw_mu']), row(lw['rw_w0']), lw['rw_w2'].astype(bf16), row(lw['rw_a0']),
                               lw['rw_a2'].astype(bf16), lw['rw_g2'].astype(bf16), row(lw['rw_k_k']), row(lw['rw_k_a']),
                               row(lw['rw_r_k']), row(lw['rw_gn_g']), row(lw['rw_gn_b']), shift0, rw_s, ncp, ns)
    conv0 = jnp.concatenate([jnp.zeros((ns, 8 - (GD_CONV - 1), GD_CONV_CH), f32), gd_conv], axis=1)
    convw = jnp.concatenate([lw['gd_conv_w'], jnp.zeros((8 - GD_CONV, GD_CONV_CH), f32)], axis=0)
    h_gd, gds1, conv1 = _gdn(cols, convw, _lane_vec(128, 12, lw['gd_a_log']), _lane_vec(16, 12, lw['gd_a_log']).T,
                             _lane_vec(128, 12, lw['gd_dt_bias']), _lane_vec(16, 12, lw['gd_dt_bias']).T,
                             lw['gd_norm'][None, :], conv0, gd_s, ncp, ns)

    x = _merge(x, cols, h_ml, h_rw, h_gd, lw['b_gates'].reshape(1, -1), lw['w_br_ml'].astype(bf16),
               lw['w_br_rw'].astype(bf16), lw['w_br_gd'].astype(bf16), lw['w_out'].astype(bf16),
               row(lw['ln1_g']), row(lw['ln1_b']))

    w_r = jnp.concatenate([lw['w_rg'], z(8 - N_GROUPS), lw['w_re'], z(128 - 8 - N_EXPERTS)], axis=1)
    b_r = jnp.concatenate([lw['b_rg'], jnp.zeros((8 - N_GROUPS,), f32), lw['b_re'],
                           jnp.zeros((128 - 8 - N_EXPERTS,), f32)])[None, :]
    x = _ffn(x, pe, w_r, b_r, lw['e_gate'], lw['e_up'], lw['e_down'],
             lw['ple_gate_w'].astype(bf16), row(lw['ple_gate_b']), lw['ple_w'].astype(bf16),
             row(lw['ln2_g']), row(lw['ln2_b']))

    new_st = (cn1[..., :ML_DV], cn1[..., ML_DV], m1[..., 0, 0], rws1, shift1[:, 7:8, :], gds1,
              conv1[:, 8 - (GD_CONV - 1):, :])
    return x, new_st


_WEIGHT_NAMES = ('w_in', 'ml_i_bias', 'ml_f_bias', 'ml_norm', 'rw_mu', 'rw_w0', 'rw_w2', 'rw_a0', 'rw_a2', 'rw_g2',
                 'rw_k_k', 'rw_k_a', 'rw_r_k', 'rw_gn_g', 'rw_gn_b', 'gd_conv_w', 'gd_a_log', 'gd_dt_bias', 'gd_norm',
                 'w_br_ml', 'w_br_rw', 'w_br_gd', 'b_gates', 'w_out', 'ln1_g', 'ln1_b', 'w_rg', 'b_rg', 'w_re', 'b_re',
                 'e_gate', 'e_up', 'e_down', 'ple_w', 'ple_gate_w', 'ple_gate_b', 'ln2_g', 'ln2_b')


def _trunk(x_prompt, x_sample, p_prompt, p_sample, states, weights):
    bp, seq, d = x_prompt.shape
    bs, dseq, _ = x_sample.shape
    assert bp == 1 and seq % CHUNK == 0 and dseq == CHUNK
    ncp, ns = seq // CHUNK, bs
    depth = p_prompt.shape[0]
    x = jnp.concatenate([x_prompt.reshape(seq, d), x_sample.reshape(bs * dseq, d)], axis=0)
    new_states = []
    for l in range(depth):
        pe = jnp.concatenate([p_prompt[l].reshape(seq, -1), p_sample[l].reshape(bs * dseq, -1)], axis=0)
        lw = {k: v[l] for k, v in weights.items()}
        x, st = _layer(x, pe, tuple(s[l] for s in states), lw, ncp, ns)
        new_states.append(st)
    y_p = x[:seq].reshape(bp, seq, d)
    y_s = x[seq:].reshape(bs, dseq, d)
    stacked = [jnp.stack([st[j] for st in new_states]) for j in range(7)]
    outs_p = [s[:, :1] for s in stacked]
    outs_s = [s[:, 1:] for s in stacked]
    return (y_p, y_s, *outs_p, *outs_s)


def kernel(x_prompt, x_sample, p_prompt, p_sample, state_mlstm_c, state_mlstm_n, state_mlstm_m, state_rwkv_s, state_rwkv_shift, state_gdn_s, state_gdn_conv, w_in, ml_i_bias, ml_f_bias, ml_norm, rw_mu, rw_w0, rw_w2, rw_a0, rw_a2, rw_g2, rw_k_k, rw_k_a, rw_r_k, rw_gn_g, rw_gn_b, gd_conv_w, gd_a_log, gd_dt_bias, gd_norm, w_br_ml, w_br_rw, w_br_gd, b_gates, w_out, ln1_g, ln1_b, w_rg, b_rg, w_re, b_re, e_gate, e_up, e_down, ple_w, ple_gate_w, ple_gate_b, ln2_g, ln2_b):
    states = (state_mlstm_c, state_mlstm_n, state_mlstm_m, state_rwkv_s, state_rwkv_shift, state_gdn_s, state_gdn_conv)
    wvals = (w_in, ml_i_bias, ml_f_bias, ml_norm, rw_mu, rw_w0, rw_w2, rw_a0, rw_a2, rw_g2, rw_k_k, rw_k_a, rw_r_k,
             rw_gn_g, rw_gn_b, gd_conv_w, gd_a_log, gd_dt_bias, gd_norm, w_br_ml, w_br_rw, w_br_gd, b_gates, w_out,
             ln1_g, ln1_b, w_rg, b_rg, w_re, b_re, e_gate, e_up, e_down, ple_w, ple_gate_w, ple_gate_b, ln2_g, ln2_b)
    return _trunk(x_prompt, x_sample, p_prompt, p_sample, states, dict(zip(_WEIGHT_NAMES, wvals)))
```

```python
import functools

import jax
import jax.numpy as jnp
from jax import lax
from jax.experimental import pallas as pl
from jax.experimental.pallas import tpu as pltpu

f32 = jnp.float32
bf16 = jnp.bfloat16

D_MODEL = 1024
DEPTH = 4
CHUNK = 64
CHUNK_SHIFT = 6
D_PLE = 256
ML_HEADS, ML_DQK, ML_DV = 4, 64, 128
ML_W = ML_HEADS * ML_DV
RW_HEADS, RW_DH = 8, 64
RW_W = RW_HEADS * RW_DH
RW_W_RANK, RW_A_RANK, RW_G_RANK = 64, 64, 128
GD_HEADS, GD_DK, GD_DV = 4, 128, 128
GD_W = GD_HEADS * GD_DV
GD_CONV = 4
N_GROUPS, EXP_PER_GROUP = 4, 4
N_EXPERTS = N_GROUPS * EXP_PER_GROUP
D_EXPERT = 256
ML_COLS = 2 * ML_HEADS * ML_DQK + 2 * ML_W + 2 * ML_HEADS
RW_COLS = 3 * RW_W + RW_W_RANK + RW_A_RANK + RW_G_RANK
GD_CONV_CH = 2 * GD_HEADS * GD_DK + GD_W
GD_COLS = GD_CONV_CH + GD_W + 2 * GD_HEADS
GATE_COLS = 3 * D_MODEL
DEEPNORM_ALPHA = (2 * DEPTH) ** 0.25
LN_EPS = 1e-5
GN_EPS = 64e-5
D_EXPERT_SHIFT = 8
assert CHUNK == 1 << CHUNK_SHIFT and D_EXPERT == 1 << D_EXPERT_SHIFT

ML_MAIN = 2 * ML_HEADS * ML_DQK + 2 * ML_W
SMALL_W = 256
GD_MAIN = GD_CONV_CH + GD_W
SEG_ML = 0
SEG_SMALL = ML_MAIN
SEG_RW = SEG_SMALL + SMALL_W
SEG_GD = 2 * GD_MAIN
SEG_GATE = SEG_GD + GD_MAIN
PROJ_COLS = SEG_GATE + GATE_COLS
assert SEG_RW == RW_COLS and SEG_GATE == 2 * GATE_COLS and SEG_GD >= SEG_RW + RW_COLS

VMEM_LIMIT = 56 * 1024 * 1024
MAX_CHUNKS_PER_STEP = {"mlstm": 4, "rwkv": 4, "gdn": 8}

_NN = (((1,), (0,)), ((), ()))
_NT = (((1,), (1,)), ((), ()))
_TN = (((0,), (0,)), ((), ()))


def _dot(a, b, dims=_NN):
    return lax.dot_general(a, b, dims, preferred_element_type=f32)


def _mm(a, b, dims=_NN):
    return _dot(a.astype(bf16), b.astype(bf16), dims)


def _split3(x):
    h1 = x.astype(bf16)
    r1 = x - h1.astype(f32)
    h2 = r1.astype(bf16)
    h3 = (r1 - h2.astype(f32)).astype(bf16)
    return h1, h2, h3


def _cumsum_rows(tri, x):
    h1, h2, h3 = _split3(x)
    return _dot(tri, h1) + (_dot(tri, h2) + _dot(tri, h3))


def _cumsum_lanes(x, tri_t):
    h1, h2, h3 = _split3(x)
    return _dot(h1, tri_t) + (_dot(h2, tri_t) + _dot(h3, tri_t))


def _softplus(x):
    return jnp.maximum(x, 0.0) + jnp.log1p(jnp.exp(-jnp.abs(x)))


def _log_sigmoid(x):
    return -_softplus(-x)


def _silu(x):
    return x * jax.nn.sigmoid(x)


def _iotas(n):
    r = lax.broadcasted_iota(jnp.int32, (n, n), 0)
    c = lax.broadcasted_iota(jnp.int32, (n, n), 1)
    return r, c


def _chunk_tris(nr):
    r, c = _iotas(nr)
    same = (r >> CHUNK_SHIFT) == (c >> CHUNK_SHIFT)
    return (same & (r >= c)).astype(bf16), (same & (r <= c)).astype(bf16)


def _tri_inv_many(mats, r, c):
    eye = (r == c).astype(f32)
    blk8 = (r >> 3) == (c >> 3)
    ns = [jnp.where(blk8, a, 0.0) for a in mats]
    nb = [n.astype(bf16) for n in ns]
    n2 = [_dot(p, p) for p in nb]
    n2b = [x.astype(bf16) for x in n2]
    n3 = [_dot(p, q) for p, q in zip(nb, n2b)]
    n4 = [_dot(q, q) for q in n2b]
    ps = [eye - n + m2 - m3 for n, m2, m3 in zip(ns, n2, n3)]
    xs = [p + _dot(p.astype(bf16), m4.astype(bf16)) for p, m4 in zip(ps, n4)]
    for sh in (3, 4, 5):
        off = ((r >> (sh + 1)) == (c >> (sh + 1))) & ((r >> sh) != (c >> sh))
        xb = [x.astype(bf16) for x in xs]
        ts = [_dot(jnp.where(off, a, 0.0).astype(bf16), q) for a, q in zip(mats, xb)]
        xs = [x - _dot(q, t.astype(bf16)) for x, q, t in zip(xs, xb, ts)]
    return xs


def _layer_norm(x, g, b):
    mu = jnp.mean(x, axis=-1, keepdims=True)
    xc = x - mu
    var = jnp.mean(xc * xc, axis=-1, keepdims=True)
    return xc * lax.rsqrt(var + LN_EPS) * g + b


def _params(n_axes):
    return pltpu.CompilerParams(dimension_semantics=("arbitrary",) * n_axes, vmem_limit_bytes=VMEM_LIMIT)


def _pick(n, cands):
    for t in cands:
        if n % t == 0:
            return t
    raise ValueError(f"no tile for {n} in {cands}")


def _chunks_per_step(n, cap):
    return _pick(n, tuple(k for k in (8, 4, 2, 1) if k <= cap))


def _inproj_body(x_ref, w_ref, o_ref, xb_s):
    @pl.when(pl.program_id(1) == 0)
    def _():
        xb_s[...] = x_ref[...].astype(bf16)

    o_ref[...] = jnp.dot(xb_s[...], w_ref[...], preferred_element_type=f32)


def _inproj(x, w):
    t, k = x.shape
    n = w.shape[1]
    tm = _pick(t, (1536, 768, 512, 384, 128))
    tn = _pick(n, (1024, 512))
    return pl.pallas_call(
        _inproj_body,
        grid=(t // tm, n // tn),
        in_specs=[pl.BlockSpec((tm, k), lambda i, j: (i, 0)), pl.BlockSpec((k, tn), lambda i, j: (0, j))],
        out_specs=pl.BlockSpec((tm, tn), lambda i, j: (i, j)),
        out_shape=jax.ShapeDtypeStruct((t, n), f32),
        scratch_shapes=[pltpu.VMEM((tm, k), bf16)],
        compiler_params=_params(2),
        name="inproj",
    )(x, w)


def _mixer_call(body, name, cols, col_specs, params, states, state_shapes, out_width, n_chunks, chunk_off):
    carry = states is None
    nch = _chunks_per_step(n_chunks, MAX_CHUNKS_PER_STEP[name])
    nr = nch * CHUNK
    assert (chunk_off * CHUNK) % nr == 0
    off = chunk_off * CHUNK // nr
    nz = lambda shp: (0,) * len(shp)
    in_specs = [pl.BlockSpec((nr, w), lambda i, cb=cb: (off + i, cb)) for w, cb in col_specs]
    in_specs += [pl.BlockSpec(p.shape, lambda i, k=p.ndim: (0,) * k) for p in params]
    args = [cols] * len(col_specs) + list(params)
    if carry:
        st_specs = [pl.BlockSpec((1,) + shp, lambda i, shp=shp: (0,) + nz(shp)) for shp in state_shapes]
        n_st = 1
        scratch = [pltpu.VMEM(shp, f32) for shp in state_shapes]
    else:
        st_specs = [pl.BlockSpec((nch,) + shp, lambda i, shp=shp: (i,) + nz(shp)) for shp in state_shapes]
        in_specs += st_specs
        args += list(states)
        n_st = n_chunks
        scratch = []
    return pl.pallas_call(
        functools.partial(body, nch=nch, carry=carry),
        grid=(n_chunks // nch,),
        in_specs=in_specs,
        out_specs=[pl.BlockSpec((nr, out_width), lambda i: (i, 0))] + st_specs,
        out_shape=[jax.ShapeDtypeStruct((n_chunks * CHUNK, out_width), bf16)]
        + [jax.ShapeDtypeStruct((n_st,) + shp, f32) for shp in state_shapes],
        scratch_shapes=scratch,
        compiler_params=_params(1),
        name=name + ("_seq" if carry else "_chunks"),
    )(*args)


def _run_mixer(body, name, cols, col_specs, params, states, state_shapes, out_width, ncp, ns):
    outs_p = _mixer_call(body, name, cols, col_specs, params, None, state_shapes, out_width, ncp, 0)
    outs_s = _mixer_call(body, name, cols, col_specs, params, states, state_shapes, out_width, ns, ncp)
    return [(outs_p[0], outs_s[0])] + [jnp.concatenate([a, b], axis=0) for a, b in zip(outs_p[1:], outs_s[1:])]


def _init_carried(carry, scratch_refs):
    if carry:
        @pl.when(pl.program_id(0) == 0)
        def _():
            for s in scratch_refs:
                s[...] = jnp.zeros_like(s)


def _shifted_rows(x, head_rows, nch, carry, shift):
    L = CHUNK
    if carry:
        full = jnp.concatenate([head_rows(0), x], axis=0)
        return full[8 - shift:8 - shift + x.shape[0]]
    parts = [jnp.concatenate([head_rows(j), x[j * L:(j + 1) * L]], axis=0)[8 - shift:8 - shift + L] for j in range(nch)]
    return jnp.concatenate(parts, axis=0) if nch > 1 else parts[0]


def _mlstm_body(*refs, nch, carry):
    if carry:
        main_ref, sm_ref, brow_ref, bcol_ref, norm_ref, h_ref, cn_out_ref, m_out_ref, cn_s, m_s = refs
    else:
        main_ref, sm_ref, brow_ref, bcol_ref, norm_ref, cn0_ref, m0_ref, h_ref, cn_out_ref, m_out_ref = refs
        cn_s = m_s = None
    L = CHUNK
    nr = nch * L
    _init_carried(carry, (cn_s, m_s))
    hs, js = range(ML_HEADS), range(nch)
    rows = [slice(j * L, (j + 1) * L) for j in js]

    r, cc = _iotas(L)
    causal = r >= cc
    tri, tri_t = _chunk_tris(nr)
    one_lane0 = (lax.broadcasted_iota(jnp.int32, (L, ML_DV), 1) == 0).astype(f32)

    sm = sm_ref[:, :128]
    g_c = sm + brow_ref[...]
    g_r = sm.T[:16, :] + bcol_ref[...]
    b_c = _cumsum_rows(tri, _log_sigmoid(g_c))
    b_r = _cumsum_lanes(_log_sigmoid(g_r), tri_t)
    cn_in = [cn_s[h] for h in hs] if carry else None
    m_in = [m_s[h] for h in hs] if carry else None

    bc = [[b_c[rows[j], 4 + h:5 + h] for h in hs] for j in js]
    d_intra = [[jnp.where(causal, bc[j][h] - b_r[4 + h:5 + h, rows[j]] + g_r[h:h + 1, rows[j]], -jnp.inf)
                for h in hs] for j in js]
    d_max = [[jnp.max(d_intra[j][h], axis=-1, keepdims=True) for h in hs] for j in js]
    b_last = [[bc[j][h][L - 1:L, :] for h in hs] for j in js]
    d_last = [[b_last[j][h] - bc[j][h] + g_c[rows[j], h:h + 1] for h in hs] for j in js]
    dl_max = [[jnp.max(d_last[j][h], axis=0, keepdims=True) for h in hs] for j in js]
    m_prev, m_new = [], []
    for j in js:
        m_prev.append((m_in if j == 0 else m_new[j - 1]) if carry else [m0_ref[j, h] for h in hs])
        m_new.append([jnp.maximum(b_last[j][h] + m_prev[j][h], dl_max[j][h]) for h in hs])
    m_t = [[jnp.maximum(bc[j][h] + m_prev[j][h], d_max[j][h]) for h in hs] for j in js]

    qb = [[(main_ref[rows[j], h * ML_DQK:(h + 1) * ML_DQK] * (ML_DQK ** -0.5)).astype(bf16) for h in hs] for j in js]
    kb = [[main_ref[rows[j], 256 + h * ML_DQK:256 + (h + 1) * ML_DQK].astype(bf16) for h in hs] for j in js]
    v = [[main_ref[rows[j], 512 + h * ML_DV:512 + (h + 1) * ML_DV] for h in hs] for j in js]
    qk = [[_dot(qb[j][h], kb[j][h], _NT) for h in hs] for j in js]
    kv = [[_dot(kb[j][h], (jnp.concatenate([v[j][h], one_lane0], axis=-1)
                           * jnp.exp(d_last[j][h] - m_new[j][h])).astype(bf16), _TN) for h in hs] for j in js]
    s = [[qk[j][h] * jnp.exp(d_intra[j][h] - m_t[j][h]) for h in hs] for j in js]
    sv = [[_mm(s[j][h], v[j][h]) for h in hs] for j in js]
    cn, cn_new = [], []
    for j in js:
        cn.append((cn_in if j == 0 else cn_new[j - 1]) if carry else [cn0_ref[j, h] for h in hs])
        cn_new.append([jnp.exp(b_last[j][h] + m_prev[j][h] - m_new[j][h]) * cn[j][h] + kv[j][h] for h in hs])
    qcn = [[_dot(qb[j][h], cn[j][h].astype(bf16)) for h in hs] for j in js]

    o_gate = jax.nn.sigmoid(main_ref[:, 1024:1024 + ML_W]) * norm_ref[...]
    jh = [(j, h) for j in js for h in hs]
    s_sum = [jnp.sum(s[j][h], axis=-1, keepdims=True) for j, h in jh]
    w_inter = [jnp.exp(bc[j][h] + m_prev[j][h] - m_t[j][h]) for j, h in jh]
    r_den = [1.0 / jnp.maximum(jnp.abs(w_inter[i] * qcn[j][h][:, ML_DV:ML_DV + 1] + s_sum[i]), jnp.exp(-m_t[j][h]))
             for i, (j, h) in enumerate(jh)]
    hh = [(w_inter[i] * qcn[j][h][:, :ML_DV] + sv[j][h]) * r_den[i] for i, (j, h) in enumerate(jh)]
    ms = [jnp.mean(x_ * x_, axis=-1, keepdims=True) for x_ in hh]
    outs = [(hh[i] * lax.rsqrt(ms[i] + 1e-6) * o_gate[rows[j], h * ML_DV:(h + 1) * ML_DV]).astype(h_ref.dtype)
            for i, (j, h) in enumerate(jh)]

    for j in js:
        for h in hs:
            h_ref[rows[j], h * ML_DV:(h + 1) * ML_DV] = outs[j * ML_HEADS + h]
    for h in hs:
        if carry:
            cn_s[h] = cn_new[nch - 1][h]
            m_s[h] = m_new[nch - 1][h]
            cn_out_ref[0, h] = cn_new[nch - 1][h]
            m_out_ref[0, h] = m_new[nch - 1][h]
        else:
            for j in js:
                cn_out_ref[j, h] = cn_new[j][h]
                m_out_ref[j, h] = m_new[j][h]


def _mlstm(cols, brow, bcol, norm, cn0, m0, ncp, ns):
    return _run_mixer(_mlstm_body, "mlstm", cols, [(ML_MAIN, SEG_ML // ML_MAIN), (SMALL_W, SEG_SMALL // SMALL_W)],
                      [brow, bcol, norm], [cn0, m0], [(ML_HEADS, ML_DQK, 2 * ML_DV), (ML_HEADS, 1, 1)], ML_W, ncp, ns)


def _gdn_body(*refs, nch, carry):
    if carry:
        (main_ref, sm_ref, convw_ref, arow_ref, acol_ref, dtrow_ref, dtcol_ref, norm_ref,
         o_ref, s_out_ref, conv_out_ref, s_s, prev_s) = refs
    else:
        (main_ref, sm_ref, convw_ref, arow_ref, acol_ref, dtrow_ref, dtcol_ref, norm_ref, s0_ref, conv0_ref,
         o_ref, s_out_ref, conv_out_ref) = refs
        s_s = prev_s = None
    L = CHUNK
    nr = nch * L
    _init_carried(carry, (s_s, prev_s))
    hs, js = range(GD_HEADS), range(nch)
    rows = [slice(j * L, (j + 1) * L) for j in js]

    r, cc = _iotas(L)
    incl = r >= cc
    strict = r > cc
    tri, tri_t = _chunk_tris(nr)

    x = main_ref[:, :GD_CONV_CH]
    s_in = [s_s[h] for h in hs] if carry else None
    head_rows = (lambda j: prev_s[...]) if carry else (lambda j: conv0_ref[j])
    conv = convw_ref[3:4, :] * x
    for d in (1, 2, 3):
        conv = conv + convw_ref[3 - d:4 - d, :] * _shifted_rows(x, head_rows, nch, carry, d)
    conv = _silu(conv)

    sm = sm_ref[:, :128]
    beta_all = jax.nn.sigmoid(sm)
    la_c = -jnp.exp(arow_ref[...]) * _softplus(sm + dtrow_ref[...])
    la_r = -jnp.exp(acol_ref[...]) * _softplus(sm.T[:16, :] + dtcol_ref[...])
    gc_c = _cumsum_rows(tri, la_c)
    gc_r = _cumsum_lanes(la_r, tri_t)

    q, k, v, beta, gcc, dec = ([[None] * GD_HEADS for _ in js] for _ in range(6))
    q_raw = [[conv[rows[j], h * GD_DK:(h + 1) * GD_DK] for h in hs] for j in js]
    k_raw = [[conv[rows[j], 512 + h * GD_DK:512 + (h + 1) * GD_DK] for h in hs] for j in js]
    q_ss = [[jnp.sum(q_raw[j][h] * q_raw[j][h], axis=-1, keepdims=True) for h in hs] for j in js]
    k_ss = [[jnp.sum(k_raw[j][h] * k_raw[j][h], axis=-1, keepdims=True) for h in hs] for j in js]
    for j in js:
        for h in hs:
            q[j][h] = q_raw[j][h] * (lax.rsqrt(q_ss[j][h] + 1e-6) * (GD_DK ** -0.5))
            k[j][h] = k_raw[j][h] * lax.rsqrt(k_ss[j][h] + 1e-6)
            v[j][h] = conv[rows[j], 1024 + h * GD_DV:1024 + (h + 1) * GD_DV]
            beta[j][h] = beta_all[rows[j], 8 + h:9 + h]
            gcc[j][h] = gc_c[rows[j], 12 + h:13 + h]
            dec[j][h] = jnp.exp(jnp.where(incl, gcc[j][h] - gc_r[12 + h:13 + h, rows[j]], -jnp.inf))
    jh = [(j, h) for j in js for h in hs]
    kb = {p: k[p[0]][p[1]].astype(bf16) for p in jh}
    kk = {p: _dot(kb[p], kb[p], _NT) for p in jh}
    qk = {p: (_dot(q[p[0]][p[1]].astype(bf16), kb[p], _NT) * dec[p[0]][p[1]]).astype(bf16) for p in jh}
    a_mat = [jnp.where(strict, beta[j][h] * kk[(j, h)] * dec[j][h], 0.0) for j, h in jh]
    rhs = {p: jnp.concatenate([v[p[0]][p[1]] * beta[p[0]][p[1]],
                               k[p[0]][p[1]] * (beta[p[0]][p[1]] * jnp.exp(gcc[p[0]][p[1]]))], axis=-1) for p in jh}
    tinv = dict(zip(jh, _tri_inv_many(a_mat, r, cc)))
    sol = {p: _mm(tinv[p], rhs[p]) for p in jh}
    qg = {p: (q[p[0]][p[1]] * jnp.exp(gcc[p[0]][p[1]])).astype(bf16) for p in jh}
    g_last = {p: gcc[p[0]][p[1]][L - 1:L, :] for p in jh}
    kd = {p: (k[p[0]][p[1]] * jnp.exp(g_last[p] - gcc[p[0]][p[1]])).astype(bf16) for p in jh}

    z_gate = _silu(main_ref[:, GD_CONV_CH:])
    s_new = []
    o_all = []
    for j in js:
        s_cur = (s_in if j == 0 else s_new[j - 1]) if carry else [s0_ref[j, h] for h in hs]
        sb = [x_.astype(bf16) for x_ in s_cur]
        ub = [(sol[(j, h)][:, :GD_DV] - _dot(sol[(j, h)][:, GD_DV:].astype(bf16), sb[h])).astype(bf16) for h in hs]
        s_new.append([jnp.exp(g_last[(j, h)]) * s_cur[h] + _dot(kd[(j, h)], ub[h], _TN) for h in hs])
        o_all.extend(_dot(qg[(j, h)], sb[h]) + _dot(qk[(j, h)], ub[h]) for h in hs)
    o_ms = [jnp.mean(x_ * x_, axis=-1, keepdims=True) for x_ in o_all]
    outs = [(o_all[i] * lax.rsqrt(o_ms[i] + 1e-6) * norm_ref[...] * z_gate[rows[j], h * GD_DV:(h + 1) * GD_DV])
            .astype(o_ref.dtype) for i, (j, h) in enumerate(jh)]

    for j in js:
        for h in hs:
            o_ref[rows[j], h * GD_DV:(h + 1) * GD_DV] = outs[j * GD_HEADS + h]
    if carry:
        tail = x[nr - 8:, :]
        prev_s[...] = tail
        conv_out_ref[0] = tail
        for h in hs:
            s_s[h] = s_new[nch - 1][h]
            s_out_ref[0, h] = s_new[nch - 1][h]
    else:
        for j in js:
            conv_out_ref[j] = x[(j + 1) * L - 8:(j + 1) * L, :]
            for h in hs:
                s_out_ref[j, h] = s_new[j][h]


def _gdn(cols, convw, arow, acol, dtrow, dtcol, norm, conv0, s0, ncp, ns):
    return _run_mixer(_gdn_body, "gdn", cols, [(GD_MAIN, SEG_GD // GD_MAIN), (SMALL_W, SEG_SMALL // SMALL_W)],
                      [convw, arow, acol, dtrow, dtcol, norm], [s0, conv0],
                      [(GD_HEADS, GD_DK, GD_DV), (8, GD_CONV_CH)], GD_W, ncp, ns)


def _rwkv_body(*refs, nch, carry):
    if carry:
        (rw_ref, mu_ref, w0_ref, w2_ref, a0_ref, a2_ref, g2_ref, kk_ref, ka_ref, rk_ref, gng_ref, gnb_ref,
         y_ref, s_out_ref, shift_out_ref, s_s, prev_s) = refs
    else:
        (rw_ref, mu_ref, w0_ref, w2_ref, a0_ref, a2_ref, g2_ref, kk_ref, ka_ref, rk_ref, gng_ref, gnb_ref,
         s0_ref, shift0_ref, y_ref, s_out_ref, shift_out_ref) = refs
        s_s = prev_s = None
    L = CHUNK
    nr = nch * L
    dh = RW_DH
    _init_carried(carry, (s_s, prev_s))
    hs, js = range(RW_HEADS), range(nch)
    rows = [slice(j * L, (j + 1) * L) for j in js]
    sls = [slice(h * dh, (h + 1) * dh) for h in hs]

    r_i, c_i = _iotas(L)
    incl = r_i >= c_i
    strict = r_i > c_i
    tri, _ = _chunk_tris(nr)

    x = rw_ref[...]
    s_in = [s_s[h] for h in hs] if carry else None
    head_rows = (lambda j: prev_s[...]) if carry else (lambda j: shift0_ref[j])
    prev = _shifted_rows(x, head_rows, nch, carry, 1)
    mixed = x + (prev - x) * mu_ref[...]
    r_all = mixed[:, 0:RW_W]
    k_all = mixed[:, RW_W:2 * RW_W]
    v_all = mixed[:, 2 * RW_W:3 * RW_W]
    w_lo = mixed[:, 3 * RW_W:3 * RW_W + RW_W_RANK]
    a_lo = mixed[:, 3 * RW_W + RW_W_RANK:3 * RW_W + RW_W_RANK + RW_A_RANK]
    g_lo = mixed[:, 3 * RW_W + RW_W_RANK + RW_A_RANK:]
    w_log = -_softplus(-(w0_ref[...] + _mm(jnp.tanh(w_lo), w2_ref[...]))) - 0.5
    lw = -jnp.exp(w_log)
    a_all = jax.nn.sigmoid(a0_ref[...] + _mm(a_lo, a2_ref[...]))
    g_all = _mm(jax.nn.sigmoid(g_lo), g2_ref[...])
    cum = _cumsum_rows(tri, lw)
    cum_ex = cum - lw

    seg_r = lax.broadcasted_iota(jnp.int32, (RW_W, 128), 0)
    seg_c = lax.broadcasted_iota(jnp.int32, (RW_W, 128), 1)
    seg = ((seg_r >> 6) == seg_c).astype(bf16)
    seg_tr = lax.broadcasted_iota(jnp.int32, (128, RW_W), 0)
    seg_tc = lax.broadcasted_iota(jnp.int32, (128, RW_W), 1)
    seg_t = (seg_tr == (seg_tc >> 6)).astype(bf16)
    head_sum = lambda a: _dot(a.astype(bf16), seg)
    head_bcast = lambda a: _dot(a.astype(bf16), seg_t)

    kk_all = k_all * kk_ref[...]
    kk_all = kk_all * head_bcast(lax.rsqrt(head_sum(kk_all * kk_all) + 1e-6))
    kmod_all = k_all * (1.0 + (a_all - 1.0) * ka_ref[...])
    e_neg = jnp.exp(-cum)
    rt_all = r_all * jnp.exp(cum)
    kt_all = kmod_all * e_neg
    at_all = -kk_all * jnp.exp(cum_ex)
    bt_all = kk_all * a_all * e_neg
    bonus_all = head_bcast(head_sum(r_all * kmod_all * rk_ref[...])) * v_all

    jh = [(j, h) for j in js for h in hs]
    piece = lambda a, p: a[rows[p[0]], sls[p[1]]]
    atb = {p: piece(at_all, p).astype(bf16) for p in jh}
    rtb = {p: piece(rt_all, p).astype(bf16) for p in jh}
    ktb = {p: piece(kt_all, p).astype(bf16) for p in jh}
    btb = {p: piece(bt_all, p).astype(bf16) for p in jh}
    vtb = {p: piece(v_all, p).T.astype(bf16) for p in jh}
    gmat = {p: _dot(jnp.concatenate([atb[p], rtb[p]], axis=0), jnp.concatenate([ktb[p], btb[p]], axis=0), _NT)
            for p in jh}
    mk = {p: jnp.concatenate([jnp.where(strict, gmat[p][:L, :L], 0.0), jnp.where(incl, gmat[p][L:, :L], 0.0)],
                             axis=0).astype(bf16) for p in jh}
    n_ab = [jnp.where(strict, -gmat[p][:L, L:], 0.0) for p in jh]
    n_rb = {p: jnp.where(incl, gmat[p][L:, L:], 0.0).astype(bf16) for p in jh}
    mkv_t = {p: _dot(vtb[p], mk[p], _NT) for p in jh}
    kv_s = {p: _dot(vtb[p], ktb[p]) for p in jh}
    tinv = {p: t.astype(bf16) for p, t in zip(jh, _tri_inv_many(n_ab, r_i, c_i))}
    tar = {p: jnp.concatenate([_dot(tinv[p], atb[p]).astype(bf16), rtb[p]], axis=0) for p in jh}
    m_t = {p: _dot(mkv_t[p][:, :L].astype(bf16), tinv[p], _NT) for p in jh}

    s_new = []
    y_rows = []
    for j in js:
        s_cur = (s_in if j == 0 else s_new[j - 1]) if carry else [s0_ref[j, h] for h in hs]
        st = [_dot(s_cur[h].astype(bf16), tar[(j, h)], _NT) for h in hs]
        ut = [(st[h][:, :L] + m_t[(j, h)]).astype(bf16) for h in hs]
        p_last = jnp.exp(cum[(j + 1) * L - 1:(j + 1) * L, :])
        s_new.append([(s_cur[h] + _dot(ut[h], btb[(j, h)]) + kv_s[(j, h)]) * p_last[:, sls[h]] for h in hs])
        yt = [st[h][:, L:] + mkv_t[(j, h)][:, L:] + _dot(ut[h], n_rb[(j, h)], _NT) for h in hs]
        y_rows.append(jnp.concatenate([yt[h].T for h in hs], axis=-1))

    y_all = jnp.concatenate(y_rows, axis=0) if nch > 1 else y_rows[0]
    mu_y = head_bcast(head_sum(y_all) * (1.0 / dh))
    yc = y_all - mu_y
    rstd = head_bcast(lax.rsqrt(head_sum(yc * yc) * (1.0 / dh) + GN_EPS))
    yn = yc * rstd * gng_ref[...] + gnb_ref[...] + bonus_all
    y_ref[...] = (yn * g_all).astype(y_ref.dtype)

    if carry:
        tail = x[nr - 8:, :]
        prev_s[...] = tail
        shift_out_ref[0] = tail
        for h in hs:
            s_s[h] = s_new[nch - 1][h]
            s_out_ref[0, h] = s_new[nch - 1][h]
    else:
        for j in js:
            shift_out_ref[j] = x[(j + 1) * L - 8:(j + 1) * L, :]
            for h in hs:
                s_out_ref[j, h] = s_new[j][h]


def _rwkv(cols, mu, w0, w2, a0, a2, g2, k_k, k_a, r_k, gn_g, gn_b, shift0, s0, ncp, ns):
    return _run_mixer(_rwkv_body, "rwkv", cols, [(RW_COLS, SEG_RW // RW_COLS)],
                      [mu, w0, w2, a0, a2, g2, k_k, k_a, r_k, gn_g, gn_b], [s0, shift0],
                      [(RW_HEADS, RW_DH, RW_DH), (8, RW_COLS)], RW_W, ncp, ns)


def _merge_body(x_ref, g_ref, hml_p, hml_s, hrw_p, hrw_s, hgd_p, hgd_s, bg_ref, wml_ref, wrw_ref, wgd_ref, wout_ref,
                lng_ref, lnb_ref, o_ref, *, n_prompt_tiles):
    d = D_MODEL
    dot = lambda a, b: jnp.dot(a, b, preferred_element_type=f32)
    in_prompt = pl.program_id(0) < n_prompt_tiles
    pick = lambda p_ref, s_ref: jnp.where(in_prompt, p_ref[...], s_ref[...])
    merged = (jax.nn.sigmoid(g_ref[:, 0:d] + bg_ref[:, 0:d]) * dot(pick(hml_p, hml_s), wml_ref[...])
              + jax.nn.sigmoid(g_ref[:, d:2 * d] + bg_ref[:, d:2 * d]) * dot(pick(hrw_p, hrw_s), wrw_ref[...])
              + jax.nn.sigmoid(g_ref[:, 2 * d:] + bg_ref[:, 2 * d:]) * dot(pick(hgd_p, hgd_s), wgd_ref[...]))
    y = dot(merged.astype(bf16), wout_ref[...])
    o_ref[...] = _layer_norm(DEEPNORM_ALPHA * x_ref[...] + y, lng_ref[...], lnb_ref[...])


def _merge(x, cols, h_ml, h_rw, h_gd, b_gates, w_ml, w_rw, w_gd, w_out, ln_g, ln_b):
    t, d = x.shape
    tp, ts = h_ml[0].shape[0], h_ml[1].shape[0]
    tm = _pick(ts, (512, 384, 128))
    assert tp % tm == 0 and tp + ts == t
    npt = tp // tm
    tok = lambda n: pl.BlockSpec((tm, n), lambda i: (i, 0))
    tok_p = lambda n: pl.BlockSpec((tm, n), lambda i: (jnp.minimum(i, npt - 1), 0))
    tok_s = lambda n: pl.BlockSpec((tm, n), lambda i: (jnp.maximum(i - npt, 0), 0))
    full = lambda a: pl.BlockSpec(a.shape, lambda i: (0, 0))
    return pl.pallas_call(
        functools.partial(_merge_body, n_prompt_tiles=npt),
        grid=(t // tm,),
        in_specs=[tok(d), pl.BlockSpec((tm, GATE_COLS), lambda i: (i, SEG_GATE // GATE_COLS)),
                  tok_p(ML_W), tok_s(ML_W), tok_p(RW_W), tok_s(RW_W), tok_p(GD_W), tok_s(GD_W),
                  full(b_gates), full(w_ml), full(w_rw), full(w_gd), full(w_out), full(ln_g), full(ln_b)],
        out_specs=tok(d),
        out_shape=jax.ShapeDtypeStruct((t, d), f32),
        compiler_params=_params(1),
        name="merge",
    )(x, cols, h_ml[0], h_ml[1], h_rw[0], h_rw[1], h_gd[0], h_gd[1], b_gates, w_ml, w_rw, w_gd, w_out, ln_g, ln_b)


def _route(lt):
    grp = [lt[i:i + 1, :] for i in range(N_GROUPS)]
    gmax = functools.reduce(jnp.maximum, grp)
    gex = [jnp.exp(g - gmax) for g in grp]
    gsum = functools.reduce(lambda a, b: a + b, gex)
    gp = [e / gsum for e in gex]
    p_sel = functools.reduce(jnp.maximum, gp)
    taken = jnp.zeros_like(p_sel)
    sel = []
    for i in range(N_GROUPS):
        s_i = jnp.where(gp[i] == p_sel, 1.0, 0.0) * (1.0 - taken)
        taken = taken + s_i
        sel.append(s_i)
    e_in = []
    for j in range(EXP_PER_GROUP):
        acc = sel[0] * lt[8 + j:9 + j, :]
        for g in range(1, N_GROUPS):
            acc = acc + sel[g] * lt[8 + g * EXP_PER_GROUP + j:9 + g * EXP_PER_GROUP + j, :]
        e_in.append(acc)
    emax = functools.reduce(jnp.maximum, e_in)
    eex = [jnp.exp(e - emax) for e in e_in]
    esum = functools.reduce(lambda a, b: a + b, eex)
    ep = [e / esum for e in eex]
    top = []
    for j in range(EXP_PER_GROUP):
        rank = jnp.zeros_like(p_sel)
        for i in range(EXP_PER_GROUP):
            if i == j:
                continue
            ahead = (ep[i] > ep[j]) if i > j else (ep[i] >= ep[j])
            rank = rank + jnp.where(ahead, 1.0, 0.0)
        top.append(jnp.where(rank < 2.0, ep[j], 0.0))
    tsum = functools.reduce(lambda a, b: a + b, top)
    wts = [tp / tsum * p_sel for tp in top]
    return [sel[g] * wts[j] for g in range(N_GROUPS) for j in range(EXP_PER_GROUP)]


def _ffn_body(x_ref, pe_ref, wr_ref, br_ref, wgu_ref, wd_ref, wpg_ref, bpg_ref, wp_ref, lng_ref, lnb_ref,
              o_ref, xb_s, gate_s, acc_s):
    e = pl.program_id(1)
    tm = x_ref.shape[0]

    @pl.when(e == 0)
    def _():
        x = x_ref[...]
        xb_s[...] = x.astype(bf16)
        logits = jnp.dot(x, wr_ref[...], precision=lax.Precision.HIGHEST, preferred_element_type=f32) + br_ref[...]
        rows = _route(logits.T)
        pad = jnp.zeros((128 - EXP_PER_GROUP, tm), f32)
        for g in range(N_GROUPS):
            gate_s[g] = jnp.concatenate(rows[g * EXP_PER_GROUP:(g + 1) * EXP_PER_GROUP] + [pad], axis=0).T

    gw = EXP_PER_GROUP * D_EXPERT
    xb = xb_s[...]
    hgu = jnp.dot(xb, wgu_ref[0], preferred_element_type=f32)
    gates = gate_s[e]
    gate_wide = jnp.concatenate([jnp.broadcast_to(gates[:, k:k + 1], (tm, D_EXPERT)) for k in range(EXP_PER_GROUP)],
                                axis=-1)
    hidden = (_silu(hgu[:, :gw]) * hgu[:, gw:] * gate_wide).astype(bf16)
    contrib = jnp.dot(hidden, wd_ref[0], preferred_element_type=f32)

    @pl.when(e == 0)
    def _():
        acc_s[...] = contrib

    @pl.when(e > 0)
    def _():
        acc_s[...] += contrib

    @pl.when(e == N_GROUPS - 1)
    def _():
        ple = (jax.nn.sigmoid(jnp.dot(xb, wpg_ref[...], preferred_element_type=f32) + bpg_ref[...])
               * jnp.dot(pe_ref[...].astype(bf16), wp_ref[...], preferred_element_type=f32))
        o_ref[...] = _layer_norm(DEEPNORM_ALPHA * x_ref[...] + acc_s[...] + ple, lng_ref[...], lnb_ref[...])


def _ffn(x, pe, w_r, b_r, e_gate, e_up, e_down, w_pg, b_pg, w_p, ln_g, ln_b):
    t, d = x.shape
    gw = EXP_PER_GROUP * D_EXPERT
    by_group = lambda w: w.reshape(N_GROUPS, EXP_PER_GROUP, d, D_EXPERT).transpose(0, 2, 1, 3).reshape(N_GROUPS, d, gw)
    w_gu = jnp.concatenate([by_group(e_gate), by_group(e_up)], axis=-1).astype(bf16)
    w_d = e_down.reshape(N_GROUPS, gw, d).astype(bf16)
    tm = _pick(t, (768, 384, 128))
    tok = lambda n: pl.BlockSpec((tm, n), lambda i, e: (i, 0))
    full = lambda a: pl.BlockSpec(a.shape, lambda i, e: (0,) * a.ndim)
    return pl.pallas_call(
        _ffn_body,
        grid=(t // tm, N_GROUPS),
        in_specs=[tok(d), tok(D_PLE), full(w_r), full(b_r),
                  pl.BlockSpec((1, d, 2 * gw), lambda i, e: (e, 0, 0)),
                  pl.BlockSpec((1, gw, d), lambda i, e: (e, 0, 0)),
                  full(w_pg), full(b_pg), full(w_p), full(ln_g), full(ln_b)],
        out_specs=tok(d),
        out_shape=jax.ShapeDtypeStruct((t, d), f32),
        scratch_shapes=[pltpu.VMEM((tm, d), bf16), pltpu.VMEM((N_GROUPS, tm, 128), f32), pltpu.VMEM((tm, d), f32)],
        compiler_params=_params(2),
        name="ffn",
    )(x, pe, w_r, b_r, w_gu, w_d, w_pg, b_pg, w_p, ln_g, ln_b)


def _lane_vec(n, off, vals):
    return jnp.zeros((1, n), f32).at[0, off:off + vals.shape[0]].set(vals)


def _layer(x, pe, st, lw, ncp, ns):
    ml_c, ml_n, ml_m, rw_s, rw_shift, gd_s, gd_conv = st
    w = lw['w_in']
    d = w.shape[0]
    z = lambda n: jnp.zeros((d, n), f32)
    w_proj = jnp.concatenate([
        w[:, 0:ML_MAIN],
        w[:, ML_MAIN:ML_COLS], w[:, ML_COLS + RW_COLS + GD_MAIN:ML_COLS + RW_COLS + GD_COLS], z(SMALL_W - 16),
        w[:, ML_COLS:ML_COLS + RW_COLS], z(SEG_GD - SEG_RW - RW_COLS),
        w[:, ML_COLS + RW_COLS:ML_COLS + RW_COLS + GD_MAIN],
        w[:, ML_COLS + RW_COLS + GD_COLS:],
    ], axis=1).astype(bf16)
    cols = _inproj(x, w_proj)

    gate_bias = jnp.concatenate([lw['ml_i_bias'], lw['ml_f_bias']])
    cn0 = jnp.concatenate([ml_c, ml_n[..., None], jnp.zeros(ml_c.shape[:-1] + (ML_DV - 1,), f32)], axis=-1)
    h_ml, cn1, m1 = _mlstm(cols, _lane_vec(128, 0, gate_bias), _lane_vec(16, 0, gate_bias).T,
                           lw['ml_norm'][None, :], cn0, ml_m[..., None, None], ncp, ns)
    row = lambda a: a.reshape(1, -1)
    shift0 = jnp.concatenate([jnp.zeros((ns, 7, RW_COLS), f32), rw_shift], axis=1)
    h_rw, rws1, shift1 = _rwkv(cols, row(lw['rw_mu']), row(lw['rw_w0']), lw['rw_w2'].astype(bf16), row(lw['rw_a0']),
                               lw['rw_a2'].astype(bf16), lw['rw_g2'].astype(bf16), row(lw['rw_k_k']), row(lw['rw_k_a']),
                               row(lw['rw_r_k']), row(lw['rw_gn_g']), row(lw['rw_gn_b']), shift0, rw_s, ncp, ns)
    conv0 = jnp.concatenate([jnp.zeros((ns, 8 - (GD_CONV - 1), GD_CONV_CH), f32), gd_conv], axis=1)
    convw = jnp.concatenate([lw['gd_conv_w'], jnp.zeros((8 - GD_CONV, GD_CONV_CH), f32)], axis=0)
    h_gd, gds1, conv1 = _gdn(cols, convw, _lane_vec(128, 12, lw['gd_a_log']), _lane_vec(16, 12, lw['gd_a_log']).T,
                             _lane_vec(128, 12, lw['gd_dt_bias']), _lane_vec(16, 12, lw['gd_dt_bias']).T,
                             lw['gd_norm'][None, :], conv0, gd_s, ncp, ns)

    x = _merge(x, cols, h_ml, h_rw, h_gd, lw['b_gates'].reshape(1, -1), lw['w_br_ml'].astype(bf16),
               lw['w_br_rw'].astype(bf16), lw['w_br_gd'].astype(bf16), lw['w_out'].astype(bf16),
               row(lw['ln1_g']), row(lw['ln1_b']))

    w_r = jnp.concatenate([lw['w_rg'], z(8 - N_GROUPS), lw['w_re'], z(128 - 8 - N_EXPERTS)], axis=1)
    b_r = jnp.concatenate([lw['b_rg'], jnp.zeros((8 - N_GROUPS,), f32), lw['b_re'],
                           jnp.zeros((128 - 8 - N_EXPERTS,), f32)])[None, :]
    x = _ffn(x, pe, w_r, b_r, lw['e_gate'], lw['e_up'], lw['e_down'],
             lw['ple_gate_w'].astype(bf16), row(lw['ple_gate_b']), lw['ple_w'].astype(bf16),
             row(lw['ln2_g']), row(lw['ln2_b']))

    new_st = (cn1[..., :ML_DV], cn1[..., ML_DV], m1[..., 0, 0], rws1, shift1[:, 7:8, :], gds1,
              conv1[:, 8 - (GD_CONV - 1):, :])
    return x, new_st


_WEIGHT_NAMES = ('w_in', 'ml_i_bias', 'ml_f_bias', 'ml_norm', 'rw_mu', 'rw_w0', 'rw_w2', 'rw_a0', 'rw_a2', 'rw_g2',
                 'rw_k_k', 'rw_k_a', 'rw_r_k', 'rw_gn_g', 'rw_gn_b', 'gd_conv_w', 'gd_a_log', 'gd_dt_bias', 'gd_norm',
                 'w_br_ml', 'w_br_rw', 'w_br_gd', 'b_gates', 'w_out', 'ln1_g', 'ln1_b', 'w_rg', 'b_rg', 'w_re', 'b_re',
                 'e_gate', 'e_up', 'e_down', 'ple_w', 'ple_gate_w', 'ple_gate_b', 'ln2_g', 'ln2_b')


def _trunk(x_prompt, x_sample, p_prompt, p_sample, states, weights):
    bp, seq, d = x_prompt.shape
    bs, dseq, _ = x_sample.shape
    assert bp == 1 and seq % CHUNK == 0 and dseq == CHUNK
    ncp, ns = seq // CHUNK, bs
    depth = p_prompt.shape[0]
    x = jnp.concatenate([x_prompt.reshape(seq, d), x_sample.reshape(bs * dseq, d)], axis=0)
    new_states = []
    for l in range(depth):
        pe = jnp.concatenate([p_prompt[l].reshape(seq, -1), p_sample[l].reshape(bs * dseq, -1)], axis=0)
        lw = {k: v[l] for k, v in weights.items()}
        x, st = _layer(x, pe, tuple(s[l] for s in states), lw, ncp, ns)
        new_states.append(st)
    y_p = x[:seq].reshape(bp, seq, d)
    y_s = x[seq:].reshape(bs, dseq, d)
    stacked = [jnp.stack([st[j] for st in new_states]) for j in range(7)]
    outs_p = [s[:, :1] for s in stacked]
    outs_s = [s[:, 1:] for s in stacked]
    return (y_p, y_s, *outs_p, *outs_s)


def kernel(x_prompt, x_sample, p_prompt, p_sample, state_mlstm_c, state_mlstm_n, state_mlstm_m, state_rwkv_s, state_rwkv_shift, state_gdn_s, state_gdn_conv, w_in, ml_i_bias, ml_f_bias, ml_norm, rw_mu, rw_w0, rw_w2, rw_a0, rw_a2, rw_g2, rw_k_k, rw_k_a, rw_r_k, rw_gn_g, rw_gn_b, gd_conv_w, gd_a_log, gd_dt_bias, gd_norm, w_br_ml, w_br_rw, w_br_gd, b_gates, w_out, ln1_g, ln1_b, w_rg, b_rg, w_re, b_re, e_gate, e_up, e_down, ple_w, ple_gate_w, ple_gate_b, ln2_g, ln2_b):
    states = (state_mlstm_c, state_mlstm_n, state_mlstm_m, state_rwkv_s, state_rwkv_shift, state_gdn_s, state_gdn_conv)
    wvals = (w_in, ml_i_bias, ml_f_bias, ml_norm, rw_mu, rw_w0, rw_w2, rw_a0, rw_a2, rw_g2, rw_k_k, rw_k_a, rw_r_k,
             rw_gn_g, rw_gn_b, gd_conv_w, gd_a_log, gd_dt_bias, gd_norm, w_br_ml, w_br_rw, w_br_gd, b_gates, w_out,
             ln1_g, ln1_b, w_rg, b_rg, w_re, b_re, e_gate, e_up, e_down, ple_w, ple_gate_w, ple_gate_b, ln2_g, ln2_b)
    return _trunk(x_prompt, x_sample, p_prompt, p_sample, states, dict(zip(_WEIGHT_NAMES, wvals)))
```
